```python
import jax
import jax.numpy as jnp
from jax import lax
import numpy as np

D_MODEL = 4096
BATCH = 8
SEQ = 2048
DEPTH = 2

GRID_W = 64
CTX_LEN = 256
HEAD_DIM = 128
N_HEADS = D_MODEL // HEAD_DIM
N_NA_HEADS = N_HEADS // 2
N_GQA_HEADS = N_HEADS - N_NA_HEADS
N_GQA_KV = N_GQA_HEADS // 4
NA_KH = 8
NA_KW = 16
Q_BLOCK = 128
ROPE_THETA = 10000.0
CONV_DIM = D_MODEL // 2
CONV_WIDTH = 3
SGU_DIM = D_MODEL // 2
SGU_GROUPS = 8
SGU_CH = SGU_DIM // SGU_GROUPS
SGU_CHUNK = 128
N_EXPERTS = 64
N_EXPERT_GROUPS = 8
TOPK_GROUPS = 4
TOP_K = 8
D_EXPERT = D_MODEL // 16
D_SHARED = D_EXPERT
ROUTED_SCALE = 2.5
MOE_BLOCK = 128
EPS = 1e-6
NEG = -1e30
N_EVEN = (DEPTH + 1) // 2
N_ODD = DEPTH // 2
NA_DIM = N_NA_HEADS * HEAD_DIM
GQA_Q_DIM = N_GQA_HEADS * HEAD_DIM
GQA_KV_DIM = N_GQA_KV * HEAD_DIM
ATTN_IN = 3 * NA_DIM + GQA_Q_DIM + 2 * GQA_KV_DIM
ATTN_OUT = NA_DIM + GQA_Q_DIM
MIX_IN = 3 * CONV_DIM + 2 * SGU_DIM
MIX_OUT = CONV_DIM + SGU_DIM

kernel_name = 'hybrid_na_gqa_conv_sgu_moe_dit'


def rms_norm(x, g):
    xf = x.astype(jnp.float32)
    y = xf * lax.rsqrt(jnp.mean(xf * xf, axis=-1, keepdims=True) + EPS)
    return (y * g.astype(jnp.float32)).astype(x.dtype)


def modulate(h, shift, scale):
    return h * (1 + scale) + shift


def rope_1d(x, pos):
    half = x.shape[-1] // 2
    freqs = ROPE_THETA ** (-jnp.arange(half, dtype=jnp.float32) / half)
    ang = pos[:, None] * freqs[None, :]
    cos = jnp.cos(ang)[:, None, :]
    sin = jnp.sin(ang)[:, None, :]
    xf = x.astype(jnp.float32)
    x1, x2 = xf[..., :half], xf[..., half:]
    return jnp.concatenate([x1 * cos - x2 * sin, x2 * cos + x1 * sin], axis=-1).astype(x.dtype)


def axial_rope(x):
    t = jnp.arange(x.shape[1])
    row = (t // GRID_W).astype(jnp.float32)
    col = (t % GRID_W).astype(jnp.float32)
    h = HEAD_DIM // 2
    return jnp.concatenate([rope_1d(x[..., :h], row), rope_1d(x[..., h:], col)], axis=-1)


def dense_attention(q, k, v):
    B, n, hq, hd = q.shape
    g = k.shape[2]
    qg = q.reshape(B, n, g, hq // g, hd)
    s = jnp.einsum('bqgrd,bkgd->bgrqk', qg, k).astype(jnp.float32) * hd ** -0.5
    p = jax.nn.softmax(s, axis=-1).astype(v.dtype)
    return jnp.einsum('bgrqk,bkgd->bqgrd', p, v).reshape(B, n, hq * hd)


def neighbourhood_attention(q, k, v, k_ctx, v_ctx, rpb):
    B, S, H, hd = q.shape
    rows = S // GRID_W
    kh = min(NA_KH, rows)
    kw = NA_KW
    nb_rows = min(kh + 1, rows)
    n_blk = rows // 2
    n_key = nb_rows * GRID_W
    kg = k.reshape(B, rows, GRID_W, H, hd)
    vg = v.reshape(B, rows, GRID_W, H, hd)
    qb = q.reshape(B, n_blk, 2 * GRID_W, H, hd).transpose(1, 0, 2, 3, 4)
    col = jnp.arange(GRID_W)
    col_start = jnp.clip(col - kw // 2, 0, GRID_W - kw)
    rpb_f = rpb.astype(jnp.float32)

    def block(args):
        b, qblk = args
        q_rows = 2 * b + jnp.arange(2)
        row_start = jnp.clip(q_rows - kh // 2, 0, rows - kh)
        band = jnp.clip(row_start[0], 0, rows - nb_rows)
        kb = lax.dynamic_slice_in_dim(kg, band, nb_rows, axis=1).reshape(B, n_key, H, hd)
        vb = lax.dynamic_slice_in_dim(vg, band, nb_rows, axis=1).reshape(B, n_key, H, hd)
        qr = jnp.repeat(q_rows, GRID_W)
        qc = jnp.tile(col, 2)
        rs = jnp.repeat(row_start, GRID_W)
        cs = jnp.tile(col_start, 2)
        kr = band + jnp.repeat(jnp.arange(nb_rows), GRID_W)
        kc = jnp.tile(col, nb_rows)
        in_win = ((kr[None, :] >= rs[:, None]) & (kr[None, :] < rs[:, None] + kh)
                  & (kc[None, :] >= cs[:, None]) & (kc[None, :] < cs[:, None] + kw))
        dr = jnp.clip(kr[None, :] - qr[:, None] + NA_KH - 1, 0, 2 * NA_KH - 2)
        dc = jnp.clip(kc[None, :] - qc[:, None] + NA_KW - 1, 0, 2 * NA_KW - 2)
        bias = rpb_f[:, dr, dc]
        s_loc = jnp.einsum('bqhd,bkhd->bhqk', qblk, kb).astype(jnp.float32) * hd ** -0.5 + bias[None]
        s_loc = jnp.where(in_win[None, None], s_loc, NEG)
        s_ctx = jnp.einsum('bqhd,bchd->bhqc', qblk, k_ctx).astype(jnp.float32) * hd ** -0.5
        p = jax.nn.softmax(jnp.concatenate([s_loc, s_ctx], axis=-1), axis=-1).astype(v.dtype)
        return (jnp.einsum('bhqk,bkhd->bqhd', p[..., :n_key], vb)
                + jnp.einsum('bhqc,bchd->bqhd', p[..., n_key:], v_ctx))

    out = lax.map(block, (jnp.arange(n_blk), qb))
    return out.transpose(1, 0, 2, 3, 4).reshape(B, S, H * hd)


def gqa_block_attention(q, k, v, k_ctx, v_ctx):
    B, S, hq, hd = q.shape
    g = k.shape[2]
    keys = jnp.concatenate([k, k_ctx], axis=1)
    vals = jnp.concatenate([v, v_ctx], axis=1)
    qb = q.reshape(B, S // Q_BLOCK, Q_BLOCK, g, hq // g, hd).transpose(1, 0, 2, 3, 4, 5)

    def block(qblk):
        s = jnp.einsum('bqgrd,bkgd->bgrqk', qblk, keys).astype(jnp.float32) * hd ** -0.5
        p = jax.nn.softmax(s, axis=-1).astype(vals.dtype)
        return jnp.einsum('bgrqk,bkgd->bqgrd', p, vals)

    out = lax.map(block, qb)
    return out.transpose(1, 0, 2, 3, 4, 5).reshape(B, S, hq * hd)


def attn_mixer(h_lat, h_ctx, w_in, w_out, rpb, q_g, k_g, update_ctx):
    splits = [NA_DIM, 2 * NA_DIM, 3 * NA_DIM, 3 * NA_DIM + GQA_Q_DIM, 3 * NA_DIM + GQA_Q_DIM + GQA_KV_DIM]

    def project(h):
        B, n, _ = h.shape
        qa, ka, va, qb, kb, vb = jnp.split(h @ w_in, splits, axis=-1)
        heads = lambda t: t.reshape(B, n, -1, HEAD_DIM)
        return (heads(qa), heads(ka), heads(va),
                rms_norm(heads(qb), q_g), rms_norm(heads(kb), k_g), heads(vb))

    qa, ka, va, qb, kb, vb = project(h_lat)
    qa_c, ka_c, va_c, qb_c, kb_c, vb_c = project(h_ctx)
    o_na = neighbourhood_attention(qa, ka, va, ka_c, va_c, rpb)
    o_gqa = gqa_block_attention(axial_rope(qb), axial_rope(kb), vb, kb_c, vb_c)
    y_lat = jnp.concatenate([o_na, o_gqa], axis=-1) @ w_out
    y_ctx = None
    if update_ctx:
        y_ctx = jnp.concatenate([dense_attention(qa_c, ka_c, va_c),
                                 dense_attention(qb_c, kb_c, vb_c)], axis=-1) @ w_out
    return y_lat, y_ctx


def conv_sgu_mixer(h, w_in, w_out, conv_w, sgu_w, sgu_b, sgu_g):
    B, n, _ = h.shape
    gb, gc, xc, u, v = jnp.split(h @ w_in, [CONV_DIM, 2 * CONV_DIM, 3 * CONV_DIM, 3 * CONV_DIM + SGU_DIM], axis=-1)
    conv = lax.conv_general_dilated(gc * xc, conv_w.reshape(CONV_WIDTH, 1, CONV_DIM), window_strides=(1,),
                                    padding=((CONV_WIDTH // 2, CONV_WIDTH // 2),),
                                    dimension_numbers=('NWC', 'WIO', 'NWC'), feature_group_count=CONV_DIM)
    y_conv = gb * conv
    vg = rms_norm(v.reshape(B, n // SGU_CHUNK, SGU_CHUNK, SGU_GROUPS, SGU_CH), sgu_g)
    mix = jnp.einsum('gts,bnsgc->bntgc', sgu_w, vg) + sgu_b.T[None, None, :, :, None]
    y_sgu = u * mix.reshape(B, n, SGU_DIM)
    return jnp.concatenate([y_conv, y_sgu], axis=-1) @ w_out


def moe_ffn(h, router_w, router_bias, w_gate, w_up, w_down, sh_gate, sh_up, sh_down):
    B, n, D = h.shape
    xt = h.reshape(-1, D)
    N = xt.shape[0]
    scores = jax.nn.sigmoid((xt @ router_w).astype(jnp.float32))
    sel = scores + router_bias.astype(jnp.float32)
    grp_score = lax.top_k(sel.reshape(N, N_EXPERT_GROUPS, -1), 2)[0].sum(-1)
    _, grp_idx = lax.top_k(grp_score, TOPK_GROUPS)
    grp_keep = jnp.any(grp_idx[:, :, None] == jnp.arange(N_EXPERT_GROUPS)[None, None, :], axis=1)
    expert_keep = jnp.repeat(grp_keep, N_EXPERTS // N_EXPERT_GROUPS, axis=1)
    _, top_idx = lax.top_k(jnp.where(expert_keep, sel, -jnp.inf), TOP_K)
    top_s = jnp.take_along_axis(scores, top_idx, axis=1)
    top_w = top_s / jnp.sum(top_s, axis=-1, keepdims=True) * ROUTED_SCALE
    gates = jnp.sum(jax.nn.one_hot(top_idx, N_EXPERTS, dtype=jnp.float32) * top_w[..., None], axis=1).astype(xt.dtype)

    def block(args):
        xb, gb = args
        hid = jax.nn.silu(jnp.einsum('td,edf->tef', xb, w_gate)) * jnp.einsum('td,edf->tef', xb, w_up)
        return jnp.einsum('tef,efd->td', hid * gb[..., None], w_down)

    routed = lax.map(block, (xt.reshape(-1, MOE_BLOCK, D), gates.reshape(-1, MOE_BLOCK, N_EXPERTS))).reshape(N, D)
    shared = (jax.nn.silu(xt @ sh_gate) * (xt @ sh_up)) @ sh_down
    return (routed + shared).reshape(B, n, D)


def setup_inputs(seed: int = 0) -> dict:
    key = jax.random.key(seed)
    ks = jax.random.split(key, 26)
    f32 = jnp.float32
    nrm = lambda k, shape, s: jax.random.normal(k, shape, f32) * s
    D = D_MODEL
    return {
        'x': nrm(ks[0], (BATCH, SEQ, D), 1.0),
        'c': nrm(ks[1], (BATCH, D), 1.0),
        'ctx': nrm(ks[2], (BATCH, CTX_LEN, D), 1.0),
        'c_ctx': nrm(ks[3], (D,), 1.0),
        'ada_w': nrm(ks[4], (DEPTH, D, 6 * D), 0.5 * D ** -0.5),
        'ada_b': nrm(ks[5], (DEPTH, 6 * D), 0.02),
        'norm_g': 1.0 + nrm(ks[6], (DEPTH, 4, D), 0.05),
        'attn_w_in': nrm(ks[7], (N_EVEN, D, ATTN_IN), D ** -0.5),
        'attn_w_out': nrm(ks[8], (N_EVEN, ATTN_OUT, D), ATTN_OUT ** -0.5),
        'na_rpb': nrm(ks[9], (N_EVEN, N_NA_HEADS, 2 * NA_KH - 1, 2 * NA_KW - 1), 0.3),
        'q_norm_g': 1.0 + nrm(ks[10], (N_EVEN, HEAD_DIM), 0.05),
        'k_norm_g': 1.0 + nrm(ks[11], (N_EVEN, HEAD_DIM), 0.05),
        'mix_w_in': nrm(ks[12], (N_ODD, D, MIX_IN), D ** -0.5),
        'mix_w_out': nrm(ks[13], (N_ODD, MIX_OUT, D), MIX_OUT ** -0.5),
        'conv_w': nrm(ks[14], (N_ODD, CONV_WIDTH, CONV_DIM), CONV_WIDTH ** -0.5),
        'sgu_w': nrm(ks[15], (N_ODD, SGU_GROUPS, SGU_CHUNK, SGU_CHUNK), SGU_CHUNK ** -0.5),
        'sgu_b': nrm(ks[16], (N_ODD, SGU_GROUPS, SGU_CHUNK), 0.02),
        'sgu_norm_g': 1.0 + nrm(ks[17], (N_ODD, SGU_GROUPS, SGU_CH), 0.05),
        'router_w': nrm(ks[18], (DEPTH, D, N_EXPERTS), D ** -0.5),
        'router_bias': nrm(ks[19], (DEPTH, N_EXPERTS), 0.01),
        'moe_w_gate': nrm(ks[20], (DEPTH, N_EXPERTS, D, D_EXPERT), D ** -0.5),
        'moe_w_up': nrm(ks[21], (DEPTH, N_EXPERTS, D, D_EXPERT), D ** -0.5),
        'moe_w_down': nrm(ks[22], (DEPTH, N_EXPERTS, D_EXPERT, D), D_EXPERT ** -0.5),
        'shared_w_gate': nrm(ks[23], (DEPTH, D, D_SHARED), D ** -0.5),
        'shared_w_up': nrm(ks[24], (DEPTH, D, D_SHARED), D ** -0.5),
        'shared_w_down': nrm(ks[25], (DEPTH, D_SHARED, D), D_SHARED ** -0.5),
    }


def reference(x, c, ctx, c_ctx, ada_w, ada_b, norm_g, attn_w_in, attn_w_out, na_rpb, q_norm_g, k_norm_g,
              mix_w_in, mix_w_out, conv_w, sgu_w, sgu_b, sgu_norm_g, router_w, router_bias,
              moe_w_gate, moe_w_up, moe_w_down, shared_w_gate, shared_w_up, shared_w_down):
    x_lat, x_ctx = x, ctx
    silu_c = jax.nn.silu(c)
    silu_cc = jax.nn.silu(c_ctx)
    for i in range(DEPTH):
        reads_ctx = i % 2 == 0
        update_ctx = any(j % 2 == 0 for j in range(i + 1, DEPTH))
        sh1, sc1, g1, sh2, sc2, g2 = jnp.split((silu_c @ ada_w[i] + ada_b[i])[:, None, :], 6, axis=-1)
        h_lat = modulate(rms_norm(x_lat, norm_g[i, 0]), sh1, sc1)
        h_ctx = None
        if reads_ctx or update_ctx:
            csh1, csc1, cg1, csh2, csc2, cg2 = jnp.split((silu_cc @ ada_w[i] + ada_b[i])[None, None, :], 6, axis=-1)
            h_ctx = modulate(rms_norm(x_ctx, norm_g[i, 0]), csh1, csc1)
        if i % 2 == 0:
            e = i // 2
            y_lat, y_ctx = attn_mixer(h_lat, h_ctx, attn_w_in[e], attn_w_out[e], na_rpb[e],
                                      q_norm_g[e], k_norm_g[e], update_ctx)
        else:
            o = i // 2
            y_lat = conv_sgu_mixer(h_lat, mix_w_in[o], mix_w_out[o], conv_w[o], sgu_w[o], sgu_b[o], sgu_norm_g[o])
            y_ctx = None
            if update_ctx:
                y_ctx = conv_sgu_mixer(h_ctx, mix_w_in[o], mix_w_out[o], conv_w[o], sgu_w[o], sgu_b[o], sgu_norm_g[o])
        moe_p = (router_w[i], router_bias[i], moe_w_gate[i], moe_w_up[i], moe_w_down[i],
                 shared_w_gate[i], shared_w_up[i], shared_w_down[i])
        x_lat = x_lat + g1 * rms_norm(y_lat, norm_g[i, 1])
        f_lat = moe_ffn(modulate(rms_norm(x_lat, norm_g[i, 2]), sh2, sc2), *moe_p)
        x_lat = x_lat + g2 * rms_norm(f_lat, norm_g[i, 3])
        if update_ctx:
            x_ctx = x_ctx + cg1 * rms_norm(y_ctx, norm_g[i, 1])
            f_ctx = moe_ffn(modulate(rms_norm(x_ctx, norm_g[i, 2]), csh2, csc2), *moe_p)
            x_ctx = x_ctx + cg2 * rms_norm(f_ctx, norm_g[i, 3])
    return x_lat
```

```python
import functools

import numpy as np
import jax
import jax.numpy as jnp
from jax import lax
from jax.experimental import pallas as pl
from jax.experimental.pallas import tpu as pltpu

F32 = jnp.float32
BF16 = jnp.bfloat16

D_MODEL = 4096
SEQ = 2048
DEPTH = 2
GRID_W = 64
GRID_ROWS = SEQ // GRID_W
CTX_LEN = 256
HEAD_DIM = 128
N_NA_HEADS = 16
N_GQA_HEADS = 16
N_GQA_KV = 4
GQA_REP = N_GQA_HEADS // N_GQA_KV
NA_KH = 8
NA_KW = 16
NA_BAND_ROWS = NA_KH + 1
NA_KEYS = NA_BAND_ROWS * GRID_W
NA_QBLK = 2 * GRID_W
NA_NBLK = GRID_ROWS // 2
ROPE_THETA = 10000.0
CONV_DIM = D_MODEL // 2
SGU_DIM = D_MODEL // 2
SGU_GROUPS = 8
SGU_CH = SGU_DIM // SGU_GROUPS
SGU_CHUNK = 128
N_EXPERTS = 64
N_EXPERT_GROUPS = 8
EXPERTS_PER_GROUP = N_EXPERTS // N_EXPERT_GROUPS
TOPK_GROUPS = 4
TOP_K = 8
D_EXPERT = D_MODEL // 16
ROUTED_SCALE = 2.5
EPS = 1e-6
NEG = -1e30
NA_DIM = N_NA_HEADS * HEAD_DIM
GQA_Q_DIM = N_GQA_HEADS * HEAD_DIM
GQA_KV_DIM = N_GQA_KV * HEAD_DIM
ATTN_SCALE = HEAD_DIM ** -0.5

V7X_VMEM_BYTES = 64 * 1024 * 1024
MIB = 1024 * 1024


def _cparams(semantics, vmem_mib):
    assert vmem_mib * MIB < V7X_VMEM_BYTES
    return pltpu.CompilerParams(dimension_semantics=semantics, vmem_limit_bytes=vmem_mib * MIB)


def _rms_scale(x):
    return lax.rsqrt(jnp.mean(x * x, axis=-1, keepdims=True) + EPS)


def _silu(x):
    return x / (1.0 + jnp.exp(-x))


def _adaln_kernel(c_ref, w_ref, b_ref, o_ref):
    a = _silu(c_ref[...])
    o_ref[0] = jnp.dot(a.astype(BF16), w_ref[0].astype(BF16), preferred_element_type=F32) + b_ref[0]


def _adaln(c_rows, ada_w, ada_b):
    depth, d, n6 = ada_w.shape
    rows = c_rows.shape[0]
    tn = 512
    return pl.pallas_call(
        _adaln_kernel,
        grid=(depth, n6 // tn),
        in_specs=[pl.BlockSpec((rows, d), lambda l, j: (0, 0)),
                  pl.BlockSpec((1, d, tn), lambda l, j: (l, 0, j)),
                  pl.BlockSpec((1, 1, tn), lambda l, j: (l, 0, j))],
        out_specs=pl.BlockSpec((1, rows, tn), lambda l, j: (l, 0, j)),
        out_shape=jax.ShapeDtypeStruct((depth, rows, n6), F32),
        compiler_params=_cparams(("parallel", "parallel"), 40),
        name="adaln",
    )(c_rows, ada_w, ada_b.reshape(depth, 1, n6))


def _norm_mod_kernel(x_ref, g_ref, sh_ref, sc_ref, o_ref):
    x = x_ref[...]
    y = (x * _rms_scale(x)) * g_ref[...]
    o_ref[...] = (y * (1.0 + sc_ref[0]) + sh_ref[0]).astype(o_ref.dtype)


def _norm_mod(x, g, sh, sc, rows_per_mod):
    m, d = x.shape
    tm = 256
    mod_spec = pl.BlockSpec((1, 1, d), lambda i: (i * tm // rows_per_mod, 0, 0))
    return pl.pallas_call(
        _norm_mod_kernel,
        grid=(m // tm,),
        in_specs=[pl.BlockSpec((tm, d), lambda i: (i, 0)),
                  pl.BlockSpec((1, d), lambda i: (0, 0)),
                  mod_spec, mod_spec],
        out_specs=pl.BlockSpec((tm, d), lambda i: (i, 0)),
        out_shape=jax.ShapeDtypeStruct((m, d), BF16),
        compiler_params=_cparams(("parallel",), 40),
        name="norm_mod",
    )(x, g.reshape(1, d), sh, sc)


def _mm_kernel(x_ref, w_ref, o_ref):
    o_ref[...] = jnp.dot(x_ref[...], w_ref[...], preferred_element_type=F32).astype(o_ref.dtype)


def _matmul(x, w, tm=1024, tn=512):
    m, k = x.shape
    n = w.shape[1]
    tm = min(tm, m)
    return pl.pallas_call(
        _mm_kernel,
        grid=(m // tm, n // tn),
        in_specs=[pl.BlockSpec((tm, k), lambda i, j: (i, 0)),
                  pl.BlockSpec((k, tn), lambda i, j: (0, j))],
        out_specs=pl.BlockSpec((tm, tn), lambda i, j: (i, j)),
        out_shape=jax.ShapeDtypeStruct((m, n), BF16),
        compiler_params=_cparams(("parallel", "arbitrary"), 48),
        name="matmul",
    )(x, w)


def _mm_post_kernel(*refs, n_parts, has_next, nk, n_chunks, row_chunk):
    lhs = refs[:n_parts]
    ws = refs[n_parts:2 * n_parts]
    if has_next:
        xres, gate, gpost, gnext, sh, sc, xout, hout = refs[2 * n_parts:]
    else:
        xres, gate, gpost, xout = refs[2 * n_parts:]
    k = pl.program_id(1)
    tm, d = xout.shape
    tn = d // n_chunks

    def part(c):
        acc = None
        for l, w in zip(lhs, ws):
            t = jnp.dot(l[...], w[:, c * tn:(c + 1) * tn], preferred_element_type=F32)
            acc = t if acc is None else acc + t
        return acc

    @pl.when(k == 0)
    def _():
        for c in range(n_chunks):
            xout[:, c * tn:(c + 1) * tn] = part(c)

    @pl.when(k > 0)
    def _():
        for c in range(n_chunks):
            xout[:, c * tn:(c + 1) * tn] += part(c)

    @pl.when(k == nk - 1)
    def _():
        def body(r, carry):
            rows = pl.ds(pl.multiple_of(r * row_chunk, row_chunk), row_chunk)
            y = xout[rows, :]
            xn = xres[rows, :] + gate[0] * ((y * _rms_scale(y)) * gpost[...])
            xout[rows, :] = xn
            if has_next:
                hn = (xn * _rms_scale(xn)) * gnext[...]
                hout[rows, :] = (hn * (1.0 + sc[0]) + sh[0]).astype(hout.dtype)
            return carry
        lax.fori_loop(0, tm // row_chunk, body, 0)


def _matmul_post(lhs_parts, w, xres, gate, gpost, nxt, tm=512):
    m, d = xres.shape
    n_parts = len(lhs_parts)
    kp = lhs_parts[0].shape[1]
    assert all(l.shape == (m, kp) for l in lhs_parts) and w.shape == (n_parts * kp, d)
    tk = 512 // n_parts
    nk = kp // tk
    has_next = nxt is not None
    vec_spec = pl.BlockSpec((1, d), lambda i, k: (0, 0))
    mod_spec = pl.BlockSpec((1, 1, d), lambda i, k: (i * tm // SEQ, 0, 0))
    in_specs = [pl.BlockSpec((tm, tk), lambda i, k: (i, k)) for _ in range(n_parts)]
    in_specs += [pl.BlockSpec((tk, d), functools.partial(lambda i, k, off: (k + off, 0), off=p * nk))
                 for p in range(n_parts)]
    in_specs += [pl.BlockSpec((tm, d), lambda i, k: (i, 0), pipeline_mode=pl.Buffered(1)), mod_spec, vec_spec]
    args = list(lhs_parts) + [w] * n_parts + [xres, gate, gpost.reshape(1, d)]
    out_specs = [pl.BlockSpec((tm, d), lambda i, k: (i, 0))]
    out_shape = [jax.ShapeDtypeStruct((m, d), F32)]
    if has_next:
        gnext, sh, sc = nxt
        in_specs += [vec_spec, mod_spec, mod_spec]
        args += [gnext.reshape(1, d), sh, sc]
        out_specs.append(pl.BlockSpec((tm, d), lambda i, k: (i, 0)))
        out_shape.append(jax.ShapeDtypeStruct((m, d), BF16))
    outs = pl.pallas_call(
        functools.partial(_mm_post_kernel, n_parts=n_parts, has_next=has_next, nk=nk, n_chunks=4, row_chunk=64),
        grid=(m // tm, nk),
        in_specs=in_specs,
        out_specs=out_specs,
        out_shape=out_shape,
        compiler_params=_cparams(("parallel", "arbitrary"), 56),
        name="matmul_post",
    )(*args)
    return (outs[0], outs[1]) if has_next else (outs[0], None)


def _qk_prep_kernel(x_ref, g_ref, cos_ref, sin_ref, o_ref, *, rope):
    n_heads = x_ref.shape[1] // HEAD_DIM
    for hh in range(n_heads):
        cols = slice(hh * HEAD_DIM, (hh + 1) * HEAD_DIM)
        x = x_ref[:, cols].astype(F32)
        y = (x * _rms_scale(x)) * g_ref[...]
        if rope:
            lane = lax.broadcasted_iota(jnp.int32, y.shape, 1)
            quarter = HEAD_DIM // 4
            partner = jnp.where(jnp.bitwise_and(lane, 2 * quarter - 1) < quarter,
                                pltpu.roll(y, HEAD_DIM - quarter, axis=1), pltpu.roll(y, quarter, axis=1))
            y = y * cos_ref[...] + partner * sin_ref[...]
        o_ref[:, cols] = y.astype(o_ref.dtype)


def _qk_prep(p, col_block0, n_col_blocks, g, cos, sin, rope):
    m = p.shape[0]
    tm, tc = 512, 4 * HEAD_DIM
    pos_blocks = SEQ // tm
    return pl.pallas_call(
        functools.partial(_qk_prep_kernel, rope=rope),
        grid=(m // tm, n_col_blocks),
        in_specs=[pl.BlockSpec((tm, tc), lambda i, j: (i, col_block0 + j)),
                  pl.BlockSpec((1, HEAD_DIM), lambda i, j: (0, 0)),
                  pl.BlockSpec((tm, HEAD_DIM), lambda i, j: (i % pos_blocks, 0)),
                  pl.BlockSpec((tm, HEAD_DIM), lambda i, j: (i % pos_blocks, 0))],
        out_specs=pl.BlockSpec((tm, tc), lambda i, j: (i, j)),
        out_shape=jax.ShapeDtypeStruct((m, tc * n_col_blocks), BF16),
        compiler_params=_cparams(("parallel", "parallel"), 32),
        name="qk_prep",
    )(p, g.reshape(1, HEAD_DIM), cos, sin)


def _rope_tables():
    half = HEAD_DIM // 4
    t = jnp.arange(SEQ)
    freqs = ROPE_THETA ** (-jnp.arange(half, dtype=F32) / half)
    ang_r = (t // GRID_W).astype(F32)[:, None] * freqs[None, :]
    ang_c = (t % GRID_W).astype(F32)[:, None] * freqs[None, :]
    cos = jnp.concatenate([jnp.cos(ang_r)] * 2 + [jnp.cos(ang_c)] * 2, axis=-1)
    sin = jnp.concatenate([-jnp.sin(ang_r), jnp.sin(ang_r), -jnp.sin(ang_c), jnp.sin(ang_c)], axis=-1)
    return cos, sin


def _na_static_tables():
    rows, kh, kw = GRID_ROWS, min(NA_KH, GRID_ROWS), NA_KW
    nb_rows = min(kh + 1, rows)
    assert nb_rows == NA_BAND_ROWS
    col = np.arange(GRID_W)
    col_start = np.clip(col - kw // 2, 0, GRID_W - kw)
    patterns, pat_of_block, bands = [], [], []
    for b in range(NA_NBLK):
        q_rows = 2 * b + np.arange(2)
        row_start = np.clip(q_rows - kh // 2, 0, rows - kh)
        band = int(np.clip(row_start[0], 0, rows - nb_rows))
        key = (tuple(q_rows - band), tuple(row_start - band))
        if key not in patterns:
            patterns.append(key)
        pat_of_block.append(patterns.index(key))
        bands.append(band)
    idx = np.full((len(patterns), NA_QBLK, NA_KEYS), -1, np.int32)
    for p, (q_rel, rs_rel) in enumerate(patterns):
        qr = np.repeat(np.array(q_rel), GRID_W)
        qc = np.tile(col, 2)
        rs = np.repeat(np.array(rs_rel), GRID_W)
        cs = np.tile(col_start, 2)
        kr = np.repeat(np.arange(nb_rows), GRID_W)
        kc = np.tile(col, nb_rows)
        in_win = ((kr[None, :] >= rs[:, None]) & (kr[None, :] < rs[:, None] + kh)
                  & (kc[None, :] >= cs[:, None]) & (kc[None, :] < cs[:, None] + kw))
        dr = np.clip(kr[None, :] - qr[:, None] + NA_KH - 1, 0, 2 * NA_KH - 2)
        dc = np.clip(kc[None, :] - qc[:, None] + NA_KW - 1, 0, 2 * NA_KW - 2)
        idx[p] = np.where(in_win, dr * (2 * NA_KW - 1) + dc, -1)
    return np.array(bands, np.int32), np.array(pat_of_block, np.int32), idx


def _na_bias_kernel(idx_ref, rpb_ref, o_ref):
    idx = idx_ref[0]
    n_tab = rpb_ref.shape[1]
    onehot = (lax.broadcasted_iota(jnp.int32, (n_tab, idx.shape[1]), 0) == idx).astype(BF16)
    r = rpb_ref[...]
    hi = r.astype(BF16)
    r1 = r - hi.astype(F32)
    mid = r1.astype(BF16)
    lo = (r1 - mid.astype(F32)).astype(BF16)
    val = (jnp.dot(hi, onehot, preferred_element_type=F32) + jnp.dot(mid, onehot, preferred_element_type=F32)
           + jnp.dot(lo, onehot, preferred_element_type=F32))
    o_ref[0] = jnp.where(idx < 0, NEG, val)


def _na_bias_table(rpb, idx):
    n_pat = idx.shape[0]
    h = rpb.shape[0]
    n_tab = 512
    flat = NA_QBLK * NA_KEYS
    tile = 8 * NA_KEYS
    rpb_flat = jnp.pad(rpb.reshape(h, -1), ((0, 0), (0, n_tab - rpb.shape[1] * rpb.shape[2])))
    out = pl.pallas_call(
        _na_bias_kernel,
        grid=(n_pat, flat // tile),
        in_specs=[pl.BlockSpec((1, 1, tile), lambda p, j: (p, 0, j)),
                  pl.BlockSpec((h, n_tab), lambda p, j: (0, 0))],
        out_specs=pl.BlockSpec((1, h, tile), lambda p, j: (p, 0, j)),
        out_shape=jax.ShapeDtypeStruct((n_pat, h, flat), F32),
        compiler_params=_cparams(("parallel", "parallel"), 32),
        name="na_bias",
    )(jnp.asarray(idx.reshape(n_pat, 1, flat)), rpb_flat)
    return out.reshape(n_pat, h, NA_QBLK, NA_KEYS)


_NT = (((1,), (1,)), ((), ()))


def _na_attn_kernel(band_ref, pat_ref, q_ref, k_ref, v_ref, kc_ref, vc_ref, bias_ref, o_ref):
    kc = kc_ref[...]
    vc = vc_ref[...]

    def body(n, carry):
        q = q_ref[pl.ds(pl.multiple_of(n * NA_QBLK, NA_QBLK), NA_QBLK), :]
        key_rows = pl.ds(pl.multiple_of(band_ref[n] * GRID_W, GRID_W), NA_KEYS)
        bias = bias_ref[pat_ref[n], 0]
        s = lax.dot_general(q, k_ref[key_rows, :], _NT, preferred_element_type=F32) * ATTN_SCALE + bias
        s = jnp.where(bias <= 0.5 * NEG, NEG, s)
        s_ctx = lax.dot_general(q, kc, _NT, preferred_element_type=F32) * ATTN_SCALE
        m = jnp.maximum(jnp.max(s, axis=-1, keepdims=True), jnp.max(s_ctx, axis=-1, keepdims=True))
        p = jnp.exp(s - m)
        p_ctx = jnp.exp(s_ctx - m)
        denom = jnp.sum(p, axis=-1, keepdims=True) + jnp.sum(p_ctx, axis=-1, keepdims=True)
        o = (jnp.dot(p.astype(BF16), v_ref[key_rows, :], preferred_element_type=F32)
             + jnp.dot(p_ctx.astype(BF16), vc, preferred_element_type=F32))
        o_ref[pl.ds(pl.multiple_of(n * NA_QBLK, NA_QBLK), NA_QBLK), :] = (o / denom).astype(o_ref.dtype)
        return carry

    lax.fori_loop(0, NA_NBLK, body, 0)


def _na_attention(p_lat, p_ctx, bias, bands, pats):
    n_tok = p_lat.shape[0]
    nb = n_tok // SEQ
    n_pat = bias.shape[0]
    hd = HEAD_DIM
    grid_spec = pltpu.PrefetchScalarGridSpec(
        num_scalar_prefetch=2,
        grid=(N_NA_HEADS, nb),
        in_specs=[pl.BlockSpec((SEQ, hd), lambda h, b, *_: (b, h)),
                  pl.BlockSpec((SEQ, hd), lambda h, b, *_: (b, N_NA_HEADS + h)),
                  pl.BlockSpec((SEQ, hd), lambda h, b, *_: (b, 2 * N_NA_HEADS + h)),
                  pl.BlockSpec((CTX_LEN, hd), lambda h, b, *_: (b, h)),
                  pl.BlockSpec((CTX_LEN, hd), lambda h, b, *_: (b, N_NA_HEADS + h)),
                  pl.BlockSpec((n_pat, 1, NA_QBLK, NA_KEYS), lambda h, b, *_: (0, h, 0, 0))],
        out_specs=pl.BlockSpec((SEQ, hd), lambda h, b, *_: (b, h)),
    )
    return pl.pallas_call(
        _na_attn_kernel,
        grid_spec=grid_spec,
        out_shape=jax.ShapeDtypeStruct((n_tok, NA_DIM), BF16),
        compiler_params=_cparams(("parallel", "parallel"), 32),
        name="na_attention",
    )(jnp.asarray(bands), jnp.asarray(pats), p_lat, p_lat, p_lat, p_ctx, p_ctx, bias)


def _gqa_attn_kernel(q_ref, k_ref, v_ref, kc_ref, vc_ref, o_ref):
    k = k_ref[...]
    v = v_ref[...]
    kc = kc_ref[...]
    vc = vc_ref[...]
    for r in range(GQA_REP):
        cols = slice(r * HEAD_DIM, (r + 1) * HEAD_DIM)
        q = q_ref[:, cols]
        s = lax.dot_general(q, k, _NT, preferred_element_type=F32) * ATTN_SCALE
        s_ctx = lax.dot_general(q, kc, _NT, preferred_element_type=F32) * ATTN_SCALE
        m = jnp.maximum(jnp.max(s, axis=-1, keepdims=True), jnp.max(s_ctx, axis=-1, keepdims=True))
        p = jnp.exp(s - m)
        p_ctx = jnp.exp(s_ctx - m)
        denom = jnp.sum(p, axis=-1, keepdims=True) + jnp.sum(p_ctx, axis=-1, keepdims=True)
        o = (jnp.dot(p.astype(BF16), v, preferred_element_type=F32)
             + jnp.dot(p_ctx.astype(BF16), vc, preferred_element_type=F32))
        o_ref[:, cols] = (o / denom).astype(o_ref.dtype)


def _gqa_attention(qg, kg, p_lat, kcg, p_ctx, v_col_block, vc_col_block, tq=256):
    n_tok = qg.shape[0]
    nb = n_tok // SEQ
    hd = HEAD_DIM
    qt = SEQ // tq
    return pl.pallas_call(
        _gqa_attn_kernel,
        grid=(nb, N_GQA_KV, qt),
        in_specs=[pl.BlockSpec((tq, GQA_REP * hd), lambda b, g, t: (b * qt + t, g)),
                  pl.BlockSpec((SEQ, hd), lambda b, g, t: (b, g)),
                  pl.BlockSpec((SEQ, hd), lambda b, g, t: (b, v_col_block + g)),
                  pl.BlockSpec((CTX_LEN, hd), lambda b, g, t: (b, g)),
                  pl.BlockSpec((CTX_LEN, hd), lambda b, g, t: (b, vc_col_block + g))],
        out_specs=pl.BlockSpec((tq, GQA_REP * hd), lambda b, g, t: (b * qt + t, g)),
        out_shape=jax.ShapeDtypeStruct((n_tok, GQA_Q_DIM), BF16),
        compiler_params=_cparams(("parallel", "parallel", "parallel"), 48),
        name="gqa_attention",
    )(qg, kg, p_lat, kcg, p_ctx)


def _mixer_kernel(gb_ref, gc_ref, xc_ref, u_ref, v_ref, cw_ref, sw_ref, sb_ref, sg_ref, o_ref):
    s, c = gb_ref.shape
    z = gc_ref[...].astype(F32) * xc_ref[...].astype(F32)
    row = lax.broadcasted_iota(jnp.int32, (s, c), 0)
    z_prev = jnp.where(row == 0, 0.0, pltpu.roll(z, 1, axis=0))
    z_next = jnp.where(row == s - 1, 0.0, pltpu.roll(z, s - 1, axis=0))
    cw = cw_ref[...]
    conv = z_prev * cw[0:1, :] + z * cw[1:2, :] + z_next * cw[2:3, :]
    o_ref[:, :c] = (gb_ref[...].astype(F32) * conv).astype(o_ref.dtype)
    w = sw_ref[0].astype(BF16)
    for n in range(s // SGU_CHUNK):
        rows = slice(n * SGU_CHUNK, (n + 1) * SGU_CHUNK)
        vv = v_ref[rows, :].astype(F32)
        vg = (vv * _rms_scale(vv)) * sg_ref[0]
        mix = jnp.dot(w, vg.astype(BF16), preferred_element_type=F32) + sb_ref[0]
        o_ref[rows, c:] = (u_ref[rows, :].astype(F32) * mix).astype(o_ref.dtype)


def _conv_sgu(p, conv_w, sgu_w, sgu_b, sgu_g):
    n_tok = p.shape[0]
    nb = n_tok // SEQ
    c = SGU_CH
    blocks = CONV_DIM // c
    col = lambda off: pl.BlockSpec((SEQ, c), functools.partial(lambda b, g, off: (b, off + g), off=off))
    return pl.pallas_call(
        _mixer_kernel,
        grid=(nb, SGU_GROUPS),
        in_specs=[col(0), col(blocks), col(2 * blocks), col(3 * blocks), col(4 * blocks),
                  pl.BlockSpec((3, c), lambda b, g: (0, g)),
                  pl.BlockSpec((1, SGU_CHUNK, SGU_CHUNK), lambda b, g: (g, 0, 0)),
                  pl.BlockSpec((1, SGU_CHUNK, 1), lambda b, g: (g, 0, 0)),
                  pl.BlockSpec((1, 1, c), lambda b, g: (g, 0, 0))],
        out_specs=pl.BlockSpec((SEQ, 2 * c), lambda b, g: (b, g)),
        out_shape=jax.ShapeDtypeStruct((n_tok, CONV_DIM + SGU_DIM), BF16),
        compiler_params=_cparams(("parallel", "parallel"), 48),
        name="conv_sgu",
    )(p, p, p, p, p, conv_w, sgu_w, sgu_b.reshape(SGU_GROUPS, SGU_CHUNK, 1), sgu_g.reshape(SGU_GROUPS, 1, c))


def _router_kernel(x_ref, whi_ref, wlo_ref, bias_ref, g_ref):
    x = x_ref[...]
    logits = (lax.dot_general(whi_ref[...], x, _NT, preferred_element_type=F32)
              + lax.dot_general(wlo_ref[...], x, _NT, preferred_element_type=F32))
    scores = 1.0 / (1.0 + jnp.exp(-logits))
    sel = scores + bias_ref[...]
    n_e, tm = sel.shape
    epg = EXPERTS_PER_GROUP
    groups = [sel[g * epg:(g + 1) * epg, :] for g in range(N_EXPERT_GROUPS)]
    gscore = []
    for sg in groups:
        top1 = jnp.max(sg, axis=0, keepdims=True)
        is_top = sg == top1
        n_top = jnp.sum(is_top.astype(F32), axis=0, keepdims=True)
        second = jnp.max(jnp.where(is_top, -jnp.inf, sg), axis=0, keepdims=True)
        gscore.append(top1 + jnp.where(n_top >= 2.0, top1, second))
    masked = []
    for g in range(N_EXPERT_GROUPS):
        rank = jnp.zeros((1, tm), F32)
        for o in range(N_EXPERT_GROUPS):
            if o == g:
                continue
            beats = (gscore[o] >= gscore[g]) if o < g else (gscore[o] > gscore[g])
            rank = rank + beats.astype(F32)
        masked.append(jnp.where(rank < float(TOPK_GROUPS), groups[g], -jnp.inf))
    v = jnp.concatenate(masked, axis=0)
    row = lax.broadcasted_iota(jnp.int32, (n_e, tm), 0)
    chosen = jnp.zeros((n_e, tm), jnp.bool_)
    for _ in range(TOP_K):
        best = jnp.max(v, axis=0, keepdims=True)
        first = jnp.min(jnp.where(v == best, row, n_e), axis=0, keepdims=True)
        pick = row == first
        chosen = jnp.logical_or(chosen, pick)
        v = jnp.where(pick, -jnp.inf, v)
    top_s = jnp.where(chosen, scores, 0.0)
    gates_t = top_s / jnp.sum(top_s, axis=0, keepdims=True) * ROUTED_SCALE
    g_ref[...] = jnp.concatenate([gates_t, jnp.zeros_like(gates_t)], axis=0).T


def _router(h, router_w, router_bias, tm=1024):
    n, d = h.shape
    wt = router_w.T
    w_hi = wt.astype(BF16)
    w_lo = (wt - w_hi.astype(F32)).astype(BF16)
    return pl.pallas_call(
        _router_kernel,
        grid=(n // tm,),
        in_specs=[pl.BlockSpec((tm, d), lambda i: (i, 0)),
                  pl.BlockSpec((N_EXPERTS, d), lambda i: (0, 0)),
                  pl.BlockSpec((N_EXPERTS, d), lambda i: (0, 0)),
                  pl.BlockSpec((N_EXPERTS, 1), lambda i: (0, 0))],
        out_specs=pl.BlockSpec((tm, 2 * N_EXPERTS), lambda i: (i, 0)),
        out_shape=jax.ShapeDtypeStruct((n, 2 * N_EXPERTS), F32),
        compiler_params=_cparams(("parallel",), 48),
        name="router",
    )(h, w_hi, w_lo, router_bias.reshape(N_EXPERTS, 1))


def _expert_kernel(x_ref, w_ref, g_ref, o_ref, *, n_routed):
    e = pl.program_id(1)
    hu = jnp.dot(x_ref[...], w_ref[0], preferred_element_type=F32)
    f = hu.shape[1] // 2
    hid = _silu(hu[:, :f]) * hu[:, f:]
    g = g_ref[...]
    lane = lax.broadcasted_iota(jnp.int32, g.shape, 1)
    gcol = jnp.sum(jnp.where(lane == e, g, 0.0), axis=1, keepdims=True)
    gcol = jnp.where(e >= n_routed, 1.0, gcol)
    o_ref[...] = (hid * gcol).astype(o_ref.dtype)


def _expert_hidden(h, w_gu, gates, tm=1024):
    n, d = h.shape
    e_all, _, f2 = w_gu.shape
    f = f2 // 2
    return pl.pallas_call(
        functools.partial(_expert_kernel, n_routed=N_EXPERTS),
        grid=(n // tm, e_all),
        in_specs=[pl.BlockSpec((tm, d), lambda i, e: (i, 0)),
                  pl.BlockSpec((1, d, f2), lambda i, e: (e, 0, 0)),
                  pl.BlockSpec((tm, gates.shape[1]), lambda i, e: (i, 0))],
        out_specs=pl.BlockSpec((tm, f), lambda i, e: (i, e)),
        out_shape=jax.ShapeDtypeStruct((n, e_all * f), BF16),
        compiler_params=_cparams(("parallel", "arbitrary"), 48),
        name="expert_hidden",
    )(h, w_gu, gates)


def _mod_parts(mods_layer, n_rows):
    return [mods_layer[:n_rows, j * D_MODEL:(j + 1) * D_MODEL].reshape(n_rows, 1, D_MODEL) for j in range(6)]


def _moe(h2, x1, gate2, gpost, nxt, router_w, router_bias, w_gate, w_up, w_down, sh_gate, sh_up, sh_down):
    e, d, f = w_gate.shape
    gates = _router(h2, router_w, router_bias)
    zeros_gu = jnp.zeros((1, d, 2 * f), BF16)
    w_gu = jnp.concatenate([
        jnp.concatenate([w_gate, w_up], axis=-1).astype(BF16),
        jnp.concatenate([sh_gate, sh_up], axis=-1).astype(BF16)[None],
        zeros_gu], axis=0)
    w_dn = jnp.concatenate([w_down.reshape(e * f, d).astype(BF16), sh_down.astype(BF16),
                            jnp.zeros((f, d), BF16)], axis=0)
    hid = _expert_hidden(h2, w_gu, gates)
    return _matmul_post([hid], w_dn, x1, gate2, gpost, nxt)


def kernel(x, c, ctx, c_ctx, ada_w, ada_b, norm_g, attn_w_in, attn_w_out, na_rpb, q_norm_g, k_norm_g, mix_w_in, mix_w_out, conv_w, sgu_w, sgu_b, sgu_norm_g, router_w, router_bias, moe_w_gate, moe_w_up, moe_w_down, shared_w_gate, shared_w_up, shared_w_down):
    nb, s, d = x.shape
    assert (s, d) == (SEQ, D_MODEL) and ctx.shape == (nb, CTX_LEN, d)
    x_lat = x.reshape(nb * s, d)
    x_ctx = ctx.reshape(nb * CTX_LEN, d)

    c_rows = jnp.concatenate([c, c_ctx[None, :], jnp.zeros((16 - nb - 1, d), F32)], axis=0)
    mods = _adaln(c_rows, ada_w, ada_b)
    lat_mods = [_mod_parts(mods[i], nb) for i in range(DEPTH)]
    ctx_mods = _mod_parts(mods[0, nb:nb + 1], 1)

    cos, sin = _rope_tables()
    bands, pats, na_idx = _na_static_tables()

    sh1, sc1 = lat_mods[0][0], lat_mods[0][1]
    h_lat = _norm_mod(x_lat, norm_g[0, 0], sh1, sc1, SEQ)

    for i in range(DEPTH):
        _, _, g1, sh2, sc2, g2 = lat_mods[i]
        if i % 2 == 0:
            e = i // 2
            w_in = attn_w_in[e]
            kv_lo, kv_hi = NA_DIM, 3 * NA_DIM
            gk_lo = 3 * NA_DIM + GQA_Q_DIM
            h_ctx = _norm_mod(x_ctx, norm_g[i, 0], ctx_mods[0], ctx_mods[1], nb * CTX_LEN)
            w_ctx = jnp.concatenate([w_in[:, kv_lo:kv_hi], w_in[:, gk_lo:]], axis=1).astype(BF16)
            p_lat = _matmul(h_lat, w_in.astype(BF16))
            p_ctx = _matmul(h_ctx, w_ctx)
            blk = 4 * HEAD_DIM
            qg = _qk_prep(p_lat, kv_hi // blk, GQA_Q_DIM // blk, q_norm_g[e], cos, sin, True)
            kg = _qk_prep(p_lat, gk_lo // blk, GQA_KV_DIM // blk, k_norm_g[e], cos, sin, True)
            kcg = _qk_prep(p_ctx, 2 * NA_DIM // blk, GQA_KV_DIM // blk, k_norm_g[e], cos, sin, False)
            bias = _na_bias_table(na_rpb[e], na_idx)
            o_na = _na_attention(p_lat, p_ctx, bias, bands, pats)
            o_gqa = _gqa_attention(qg, kg, p_lat, kcg, p_ctx,
                                   (gk_lo + GQA_KV_DIM) // HEAD_DIM, (2 * NA_DIM + GQA_KV_DIM) // HEAD_DIM)
            parts = [o_na, o_gqa]
            w_out = attn_w_out[e].astype(BF16)
        else:
            o = i // 2
            p_mix = _matmul(h_lat, mix_w_in[o].astype(BF16))
            parts = [_conv_sgu(p_mix, conv_w[o], sgu_w[o], sgu_b[o], sgu_norm_g[o])]
            wo = mix_w_out[o]
            w_out = jnp.concatenate(
                [wo[:CONV_DIM].reshape(SGU_GROUPS, SGU_CH, d), wo[CONV_DIM:].reshape(SGU_GROUPS, SGU_CH, d)],
                axis=1).reshape(CONV_DIM + SGU_DIM, d).astype(BF16)
        x1, h2 = _matmul_post(parts, w_out, x_lat, g1, norm_g[i, 1], (norm_g[i, 2], sh2, sc2))
        nxt = None
        if i + 1 < DEPTH:
            nxt = (norm_g[i + 1, 0], lat_mods[i + 1][0], lat_mods[i + 1][1])
        x_lat, h_lat = _moe(h2, x1, g2, norm_g[i, 3], nxt, router_w[i], router_bias[i],
                            moe_w_gate[i], moe_w_up[i], moe_w_down[i],
                            shared_w_gate[i], shared_w_up[i], shared_w_down[i])
    return x_lat.reshape(nb, s, d)
```

```python
import functools

import numpy as np
import jax
import jax.numpy as jnp
from jax import lax
from jax.experimental import pallas as pl
from jax.experimental.pallas import tpu as pltpu

F32 = jnp.float32
BF16 = jnp.bfloat16

D_MODEL = 4096
SEQ = 2048
DEPTH = 2
GRID_W = 64
GRID_ROWS = SEQ // GRID_W
CTX_LEN = 256
HEAD_DIM = 128
N_NA_HEADS = 16
N_GQA_HEADS = 16
N_GQA_KV = 4
GQA_REP = N_GQA_HEADS // N_GQA_KV
NA_KH = 8
NA_KW = 16
NA_BAND_ROWS = NA_KH + 1
NA_KEYS = NA_BAND_ROWS * GRID_W
NA_QBLK = 2 * GRID_W
NA_NBLK = GRID_ROWS // 2
ROPE_THETA = 10000.0
CONV_DIM = D_MODEL // 2
SGU_DIM = D_MODEL // 2
SGU_GROUPS = 8
SGU_CH = SGU_DIM // SGU_GROUPS
SGU_CHUNK = 128
N_EXPERTS = 64
N_EXPERT_GROUPS = 8
EXPERTS_PER_GROUP = N_EXPERTS // N_EXPERT_GROUPS
TOPK_GROUPS = 4
TOP_K = 8
D_EXPERT = D_MODEL // 16
ROUTED_SCALE = 2.5
EPS = 1e-6
NEG = -1e30
NA_DIM = N_NA_HEADS * HEAD_DIM
GQA_Q_DIM = N_GQA_HEADS * HEAD_DIM
GQA_KV_DIM = N_GQA_KV * HEAD_DIM
ATTN_SCALE = HEAD_DIM ** -0.5
LANES = 128
REC_ROWS = D_MODEL // 2 // LANES
MOE_TILE = 256
GATHER_PITCH = 24
COMBINE_TOKENS = 64

V7X_VMEM_BYTES = 64 * 1024 * 1024
MIB = 1024 * 1024


def _cparams(semantics, vmem_mib):
    assert vmem_mib * MIB < V7X_VMEM_BYTES
    return pltpu.CompilerParams(dimension_semantics=semantics, vmem_limit_bytes=vmem_mib * MIB)


def _rms_scale(x):
    return lax.rsqrt(jnp.mean(x * x, axis=-1, keepdims=True) + EPS)


def _silu(x):
    return x / (1.0 + jnp.exp(-x))


def _adaln_kernel(c_ref, w_ref, b_ref, o_ref):
    a = _silu(c_ref[...])
    o_ref[0] = jnp.dot(a.astype(BF16), w_ref[0].astype(BF16), preferred_element_type=F32) + b_ref[0]


def _adaln(c_rows, ada_w, ada_b):
    depth, d, n6 = ada_w.shape
    rows = c_rows.shape[0]
    tn = 512
    return pl.pallas_call(
        _adaln_kernel,
        grid=(depth, n6 // tn),
        in_specs=[pl.BlockSpec((rows, d), lambda l, j: (0, 0)),
                  pl.BlockSpec((1, d, tn), lambda l, j: (l, 0, j)),
                  pl.BlockSpec((1, 1, tn), lambda l, j: (l, 0, j))],
        out_specs=pl.BlockSpec((1, rows, tn), lambda l, j: (l, 0, j)),
        out_shape=jax.ShapeDtypeStruct((depth, rows, n6), F32),
        compiler_params=_cparams(("parallel", "parallel"), 40),
        name="adaln",
    )(c_rows, ada_w, ada_b.reshape(depth, 1, n6))


def _norm_mod_kernel(x_ref, g_ref, sh_ref, sc_ref, o_ref):
    x = x_ref[...]
    y = (x * _rms_scale(x)) * g_ref[...]
    o_ref[...] = (y * (1.0 + sc_ref[0]) + sh_ref[0]).astype(o_ref.dtype)


def _norm_mod(x, g, sh, sc, rows_per_mod):
    m, d = x.shape
    tm = 256
    mod_spec = pl.BlockSpec((1, 1, d), lambda i: (i * tm // rows_per_mod, 0, 0))
    return pl.pallas_call(
        _norm_mod_kernel,
        grid=(m // tm,),
        in_specs=[pl.BlockSpec((tm, d), lambda i: (i, 0)),
                  pl.BlockSpec((1, d), lambda i: (0, 0)),
                  mod_spec, mod_spec],
        out_specs=pl.BlockSpec((tm, d), lambda i: (i, 0)),
        out_shape=jax.ShapeDtypeStruct((m, d), BF16),
        compiler_params=_cparams(("parallel",), 40),
        name="norm_mod",
    )(x, g.reshape(1, d), sh, sc)


def _mm_kernel(x_ref, w_ref, o_ref):
    o_ref[...] = jnp.dot(x_ref[...], w_ref[...], preferred_element_type=F32).astype(o_ref.dtype)


def _matmul(x, w, tm=1024, tn=512):
    m, k = x.shape
    n = w.shape[1]
    tm = min(tm, m)
    return pl.pallas_call(
        _mm_kernel,
        grid=(m // tm, n // tn),
        in_specs=[pl.BlockSpec((tm, k), lambda i, j: (i, 0)),
                  pl.BlockSpec((k, tn), lambda i, j: (0, j))],
        out_specs=pl.BlockSpec((tm, tn), lambda i, j: (i, j)),
        out_shape=jax.ShapeDtypeStruct((m, n), BF16),
        compiler_params=_cparams(("parallel", "arbitrary"), 48),
        name="matmul",
    )(x, w)


def _pack_bf16_pairs(lo, hi):
    lo_bits = pltpu.bitcast(lo.astype(BF16).astype(F32), jnp.uint32)
    hi_bits = pltpu.bitcast(hi.astype(BF16).astype(F32), jnp.uint32)
    return jnp.bitwise_or(jnp.right_shift(lo_bits, np.uint32(16)), hi_bits)


def _unpack_bf16_pairs(words):
    lo = pltpu.bitcast(jnp.left_shift(words, np.uint32(16)), F32)
    hi = pltpu.bitcast(jnp.bitwise_and(words, np.uint32(0xFFFF0000)), F32)
    return lo, hi


def _store_records(ref, first_row, words, pitch):
    for s in range(REC_ROWS):
        ref[pl.ds(first_row + s, words.shape[0], stride=pitch), :] = words[:, s * 128:(s + 1) * 128]


def _load_records(ref, first_row, n, pitch):
    return jnp.concatenate([ref[pl.ds(first_row + s, n, stride=pitch), :] for s in range(REC_ROWS)], axis=1)


def _mm_post_kernel(*refs, n_parts, nk, n_chunks, row_chunk):
    lhs = refs[:n_parts]
    ws = refs[n_parts:2 * n_parts]
    xres, gate, gpost, gnext, sh, sc, xout, hout, hrec = refs[2 * n_parts:]
    k = pl.program_id(1)
    tm, d = xout.shape
    tn = d // n_chunks

    def part(c):
        acc = None
        for l, w in zip(lhs, ws):
            t = jnp.dot(l[...], w[:, c * tn:(c + 1) * tn], preferred_element_type=F32)
            acc = t if acc is None else acc + t
        return acc

    @pl.when(k == 0)
    def _():
        for c in range(n_chunks):
            xout[:, c * tn:(c + 1) * tn] = part(c)

    @pl.when(k > 0)
    def _():
        for c in range(n_chunks):
            xout[:, c * tn:(c + 1) * tn] += part(c)

    @pl.when(k == nk - 1)
    def _():
        def body(r, carry):
            rows = pl.ds(pl.multiple_of(r * row_chunk, row_chunk), row_chunk)
            y = xout[rows, :]
            xn = xres[rows, :] + gate[0] * ((y * _rms_scale(y)) * gpost[...])
            xout[rows, :] = xn
            h = ((xn * _rms_scale(xn)) * gnext[...]) * (1.0 + sc[0]) + sh[0]
            hout[rows, :] = h.astype(hout.dtype)
            half = d // 2
            first = pl.multiple_of(r * (row_chunk * REC_ROWS), row_chunk * REC_ROWS)
            _store_records(hrec, first, _pack_bf16_pairs(h[:, :half], h[:, half:]), REC_ROWS)
            return carry
        lax.fori_loop(0, tm // row_chunk, body, 0)


def _matmul_post(lhs_parts, w, xres, gate, gpost, gnext, sh, sc, tm=512):
    m, d = xres.shape
    n_parts = len(lhs_parts)
    kp = lhs_parts[0].shape[1]
    assert all(l.shape == (m, kp) for l in lhs_parts) and w.shape == (n_parts * kp, d)
    tk = 512 // n_parts
    nk = kp // tk
    vec_spec = pl.BlockSpec((1, d), lambda i, k: (0, 0))
    mod_spec = pl.BlockSpec((1, 1, d), lambda i, k: (i * tm // SEQ, 0, 0))
    row_spec = pl.BlockSpec((tm, d), lambda i, k: (i, 0))
    in_specs = [pl.BlockSpec((tm, tk), lambda i, k: (i, k)) for _ in range(n_parts)]
    in_specs += [pl.BlockSpec((tk, d), functools.partial(lambda i, k, off: (k + off, 0), off=p * nk))
                 for p in range(n_parts)]
    in_specs += [pl.BlockSpec((tm, d), lambda i, k: (i, 0), pipeline_mode=pl.Buffered(1)),
                 mod_spec, vec_spec, vec_spec, mod_spec, mod_spec]
    args = list(lhs_parts) + [w] * n_parts + [xres, gate, gpost.reshape(1, d), gnext.reshape(1, d), sh, sc]
    return pl.pallas_call(
        functools.partial(_mm_post_kernel, n_parts=n_parts, nk=nk, n_chunks=4, row_chunk=64),
        grid=(m // tm, nk),
        in_specs=in_specs,
        out_specs=[row_spec, row_spec, pl.BlockSpec((tm * REC_ROWS, 128), lambda i, k: (i, 0))],
        out_shape=[jax.ShapeDtypeStruct((m, d), F32), jax.ShapeDtypeStruct((m, d), BF16),
                   jax.ShapeDtypeStruct((m * REC_ROWS, 128), jnp.uint32)],
        compiler_params=_cparams(("parallel", "arbitrary"), 56),
        name="matmul_post",
    )(*args)


def _qk_prep_kernel(x_ref, g_ref, cos_ref, sin_ref, o_ref, *, rope):
    n_heads = x_ref.shape[1] // HEAD_DIM
    for hh in range(n_heads):
        cols = slice(hh * HEAD_DIM, (hh + 1) * HEAD_DIM)
        x = x_ref[:, cols].astype(F32)
        y = (x * _rms_scale(x)) * g_ref[...]
        if rope:
            lane = lax.broadcasted_iota(jnp.int32, y.shape, 1)
            quarter = HEAD_DIM // 4
            partner = jnp.where(jnp.bitwise_and(lane, 2 * quarter - 1) < quarter,
                                pltpu.roll(y, HEAD_DIM - quarter, axis=1), pltpu.roll(y, quarter, axis=1))
            y = y * cos_ref[...] + partner * sin_ref[...]
        o_ref[:, cols] = y.astype(o_ref.dtype)


def _qk_prep(p, col_block0, n_col_blocks, g, cos, sin, rope):
    m = p.shape[0]
    tm, tc = 512, 4 * HEAD_DIM
    pos_blocks = SEQ // tm
    return pl.pallas_call(
        functools.partial(_qk_prep_kernel, rope=rope),
        grid=(m // tm, n_col_blocks),
        in_specs=[pl.BlockSpec((tm, tc), lambda i, j: (i, col_block0 + j)),
                  pl.BlockSpec((1, HEAD_DIM), lambda i, j: (0, 0)),
                  pl.BlockSpec((tm, HEAD_DIM), lambda i, j: (i % pos_blocks, 0)),
                  pl.BlockSpec((tm, HEAD_DIM), lambda i, j: (i % pos_blocks, 0))],
        out_specs=pl.BlockSpec((tm, tc), lambda i, j: (i, j)),
        out_shape=jax.ShapeDtypeStruct((m, tc * n_col_blocks), BF16),
        compiler_params=_cparams(("parallel", "parallel"), 32),
        name="qk_prep",
    )(p, g.reshape(1, HEAD_DIM), cos, sin)


def _rope_tables():
    half = HEAD_DIM // 4
    t = jnp.arange(SEQ)
    freqs = ROPE_THETA ** (-jnp.arange(half, dtype=F32) / half)
    ang_r = (t // GRID_W).astype(F32)[:, None] * freqs[None, :]
    ang_c = (t % GRID_W).astype(F32)[:, None] * freqs[None, :]
    cos = jnp.concatenate([jnp.cos(ang_r)] * 2 + [jnp.cos(ang_c)] * 2, axis=-1)
    sin = jnp.concatenate([-jnp.sin(ang_r), jnp.sin(ang_r), -jnp.sin(ang_c), jnp.sin(ang_c)], axis=-1)
    return cos, sin


def _na_static_tables():
    rows, kh, kw = GRID_ROWS, min(NA_KH, GRID_ROWS), NA_KW
    nb_rows = min(kh + 1, rows)
    assert nb_rows == NA_BAND_ROWS
    col = np.arange(GRID_W)
    col_start = np.clip(col - kw // 2, 0, GRID_W - kw)
    patterns, pat_of_block, bands = [], [], []
    for b in range(NA_NBLK):
        q_rows = 2 * b + np.arange(2)
        row_start = np.clip(q_rows - kh // 2, 0, rows - kh)
        band = int(np.clip(row_start[0], 0, rows - nb_rows))
        key = (tuple(q_rows - band), tuple(row_start - band))
        if key not in patterns:
            patterns.append(key)
        pat_of_block.append(patterns.index(key))
        bands.append(band)
    idx = np.full((len(patterns), NA_QBLK, NA_KEYS), -1, np.int32)
    for p, (q_rel, rs_rel) in enumerate(patterns):
        qr = np.repeat(np.array(q_rel), GRID_W)
        qc = np.tile(col, 2)
        rs = np.repeat(np.array(rs_rel), GRID_W)
        cs = np.tile(col_start, 2)
        kr = np.repeat(np.arange(nb_rows), GRID_W)
        kc = np.tile(col, nb_rows)
        in_win = ((kr[None, :] >= rs[:, None]) & (kr[None, :] < rs[:, None] + kh)
                  & (kc[None, :] >= cs[:, None]) & (kc[None, :] < cs[:, None] + kw))
        dr = np.clip(kr[None, :] - qr[:, None] + NA_KH - 1, 0, 2 * NA_KH - 2)
        dc = np.clip(kc[None, :] - qc[:, None] + NA_KW - 1, 0, 2 * NA_KW - 2)
        idx[p] = np.where(in_win, dr * (2 * NA_KW - 1) + dc, -1)
    return np.array(bands, np.int32), np.array(pat_of_block, np.int32), idx


def _na_bias_kernel(idx_ref, rpb_ref, o_ref):
    idx = idx_ref[0]
    n_tab = rpb_ref.shape[1]
    onehot = (lax.broadcasted_iota(jnp.int32, (n_tab, idx.shape[1]), 0) == idx).astype(BF16)
    r = rpb_ref[...]
    hi = r.astype(BF16)
    r1 = r - hi.astype(F32)
    mid = r1.astype(BF16)
    lo = (r1 - mid.astype(F32)).astype(BF16)
    val = (jnp.dot(hi, onehot, preferred_element_type=F32) + jnp.dot(mid, onehot, preferred_element_type=F32)
           + jnp.dot(lo, onehot, preferred_element_type=F32))
    o_ref[0] = jnp.where(idx < 0, NEG, val)


def _na_bias_table(rpb, idx):
    n_pat = idx.shape[0]
    h = rpb.shape[0]
    n_tab = 512
    flat = NA_QBLK * NA_KEYS
    tile = 8 * NA_KEYS
    rpb_flat = jnp.pad(rpb.reshape(h, -1), ((0, 0), (0, n_tab - rpb.shape[1] * rpb.shape[2])))
    out = pl.pallas_call(
        _na_bias_kernel,
        grid=(n_pat, flat // tile),
        in_specs=[pl.BlockSpec((1, 1, tile), lambda p, j: (p, 0, j)),
                  pl.BlockSpec((h, n_tab), lambda p, j: (0, 0))],
        out_specs=pl.BlockSpec((1, h, tile), lambda p, j: (p, 0, j)),
        out_shape=jax.ShapeDtypeStruct((n_pat, h, flat), F32),
        compiler_params=_cparams(("parallel", "parallel"), 32),
        name="na_bias",
    )(jnp.asarray(idx.reshape(n_pat, 1, flat)), rpb_flat)
    return out.reshape(n_pat, h, NA_QBLK, NA_KEYS)


_NT = (((1,), (1,)), ((), ()))


def _na_attn_kernel(band_ref, pat_ref, q_ref, k_ref, v_ref, kc_ref, vc_ref, bias_ref, o_ref):
    kc = kc_ref[...]
    vc = vc_ref[...]

    def body(n, carry):
        q = q_ref[pl.ds(pl.multiple_of(n * NA_QBLK, NA_QBLK), NA_QBLK), :]
        key_rows = pl.ds(pl.multiple_of(band_ref[n] * GRID_W, GRID_W), NA_KEYS)
        bias = bias_ref[pat_ref[n], 0]
        s = lax.dot_general(q, k_ref[key_rows, :], _NT, preferred_element_type=F32) * ATTN_SCALE + bias
        s = jnp.where(bias <= 0.5 * NEG, NEG, s)
        s_ctx = lax.dot_general(q, kc, _NT, preferred_element_type=F32) * ATTN_SCALE
        m = jnp.maximum(jnp.max(s, axis=-1, keepdims=True), jnp.max(s_ctx, axis=-1, keepdims=True))
        p = jnp.exp(s - m)
        p_ctx = jnp.exp(s_ctx - m)
        denom = jnp.sum(p, axis=-1, keepdims=True) + jnp.sum(p_ctx, axis=-1, keepdims=True)
        o = (jnp.dot(p.astype(BF16), v_ref[key_rows, :], preferred_element_type=F32)
             + jnp.dot(p_ctx.astype(BF16), vc, preferred_element_type=F32))
        o_ref[pl.ds(pl.multiple_of(n * NA_QBLK, NA_QBLK), NA_QBLK), :] = (o / denom).astype(o_ref.dtype)
        return carry

    lax.fori_loop(0, NA_NBLK, body, 0)


def _na_attention(p_lat, p_ctx, bias, bands, pats):
    n_tok = p_lat.shape[0]
    nb = n_tok // SEQ
    n_pat = bias.shape[0]
    hd = HEAD_DIM
    grid_spec = pltpu.PrefetchScalarGridSpec(
        num_scalar_prefetch=2,
        grid=(N_NA_HEADS, nb),
        in_specs=[pl.BlockSpec((SEQ, hd), lambda h, b, *_: (b, h)),
                  pl.BlockSpec((SEQ, hd), lambda h, b, *_: (b, N_NA_HEADS + h)),
                  pl.BlockSpec((SEQ, hd), lambda h, b, *_: (b, 2 * N_NA_HEADS + h)),
                  pl.BlockSpec((CTX_LEN, hd), lambda h, b, *_: (b, h)),
                  pl.BlockSpec((CTX_LEN, hd), lambda h, b, *_: (b, N_NA_HEADS + h)),
                  pl.BlockSpec((n_pat, 1, NA_QBLK, NA_KEYS), lambda h, b, *_: (0, h, 0, 0))],
        out_specs=pl.BlockSpec((SEQ, hd), lambda h, b, *_: (b, h)),
    )
    return pl.pallas_call(
        _na_attn_kernel,
        grid_spec=grid_spec,
        out_shape=jax.ShapeDtypeStruct((n_tok, NA_DIM), BF16),
        compiler_params=_cparams(("parallel", "parallel"), 32),
        name="na_attention",
    )(jnp.asarray(bands), jnp.asarray(pats), p_lat, p_lat, p_lat, p_ctx, p_ctx, bias)


def _gqa_attn_kernel(q_ref, k_ref, v_ref, kc_ref, vc_ref, o_ref):
    k = k_ref[...]
    v = v_ref[...]
    kc = kc_ref[...]
    vc = vc_ref[...]
    for r in range(GQA_REP):
        cols = slice(r * HEAD_DIM, (r + 1) * HEAD_DIM)
        q = q_ref[:, cols]
        s = lax.dot_general(q, k, _NT, preferred_element_type=F32) * ATTN_SCALE
        s_ctx = lax.dot_general(q, kc, _NT, preferred_element_type=F32) * ATTN_SCALE
        m = jnp.maximum(jnp.max(s, axis=-1, keepdims=True), jnp.max(s_ctx, axis=-1, keepdims=True))
        p = jnp.exp(s - m)
        p_ctx = jnp.exp(s_ctx - m)
        denom = jnp.sum(p, axis=-1, keepdims=True) + jnp.sum(p_ctx, axis=-1, keepdims=True)
        o = (jnp.dot(p.astype(BF16), v, preferred_element_type=F32)
             + jnp.dot(p_ctx.astype(BF16), vc, preferred_element_type=F32))
        o_ref[:, cols] = (o / denom).astype(o_ref.dtype)


def _gqa_attention(qg, kg, p_lat, kcg, p_ctx, v_col_block, vc_col_block, tq=256):
    n_tok = qg.shape[0]
    nb = n_tok // SEQ
    hd = HEAD_DIM
    qt = SEQ // tq
    return pl.pallas_call(
        _gqa_attn_kernel,
        grid=(nb, N_GQA_KV, qt),
        in_specs=[pl.BlockSpec((tq, GQA_REP * hd), lambda b, g, t: (b * qt + t, g)),
                  pl.BlockSpec((SEQ, hd), lambda b, g, t: (b, g)),
                  pl.BlockSpec((SEQ, hd), lambda b, g, t: (b, v_col_block + g)),
                  pl.BlockSpec((CTX_LEN, hd), lambda b, g, t: (b, g)),
                  pl.BlockSpec((CTX_LEN, hd), lambda b, g, t: (b, vc_col_block + g))],
        out_specs=pl.BlockSpec((tq, GQA_REP * hd), lambda b, g, t: (b * qt + t, g)),
        out_shape=jax.ShapeDtypeStruct((n_tok, GQA_Q_DIM), BF16),
        compiler_params=_cparams(("parallel", "parallel", "parallel"), 48),
        name="gqa_attention",
    )(qg, kg, p_lat, kcg, p_ctx)


def _mixer_kernel(gb_ref, gc_ref, xc_ref, u_ref, v_ref, cw_ref, sw_ref, sb_ref, sg_ref, o_ref):
    s, c = gb_ref.shape
    z = gc_ref[...].astype(F32) * xc_ref[...].astype(F32)
    row = lax.broadcasted_iota(jnp.int32, (s, c), 0)
    z_prev = jnp.where(row == 0, 0.0, pltpu.roll(z, 1, axis=0))
    z_next = jnp.where(row == s - 1, 0.0, pltpu.roll(z, s - 1, axis=0))
    cw = cw_ref[...]
    conv = z_prev * cw[0:1, :] + z * cw[1:2, :] + z_next * cw[2:3, :]
    o_ref[:, :c] = (gb_ref[...].astype(F32) * conv).astype(o_ref.dtype)
    w = sw_ref[0].astype(BF16)
    for n in range(s // SGU_CHUNK):
        rows = slice(n * SGU_CHUNK, (n + 1) * SGU_CHUNK)
        vv = v_ref[rows, :].astype(F32)
        vg = (vv * _rms_scale(vv)) * sg_ref[0]
        mix = jnp.dot(w, vg.astype(BF16), preferred_element_type=F32) + sb_ref[0]
        o_ref[rows, c:] = (u_ref[rows, :].astype(F32) * mix).astype(o_ref.dtype)


def _conv_sgu(p, conv_w, sgu_w, sgu_b, sgu_g):
    n_tok = p.shape[0]
    nb = n_tok // SEQ
    c = SGU_CH
    blocks = CONV_DIM // c
    col = lambda off: pl.BlockSpec((SEQ, c), functools.partial(lambda b, g, off: (b, off + g), off=off))
    return pl.pallas_call(
        _mixer_kernel,
        grid=(nb, SGU_GROUPS),
        in_specs=[col(0), col(blocks), col(2 * blocks), col(3 * blocks), col(4 * blocks),
                  pl.BlockSpec((3, c), lambda b, g: (0, g)),
                  pl.BlockSpec((1, SGU_CHUNK, SGU_CHUNK), lambda b, g: (g, 0, 0)),
                  pl.BlockSpec((1, SGU_CHUNK, 1), lambda b, g: (g, 0, 0)),
                  pl.BlockSpec((1, 1, c), lambda b, g: (g, 0, 0))],
        out_specs=pl.BlockSpec((SEQ, 2 * c), lambda b, g: (b, g)),
        out_shape=jax.ShapeDtypeStruct((n_tok, CONV_DIM + SGU_DIM), BF16),
        compiler_params=_cparams(("parallel", "parallel"), 48),
        name="conv_sgu",
    )(p, p, p, p, p, conv_w, sgu_w, sgu_b.reshape(SGU_GROUPS, SGU_CHUNK, 1), sgu_g.reshape(SGU_GROUPS, 1, c))


def _router_kernel(x_ref, whi_ref, wlo_ref, bias_ref, tri_ref, wsh_ref,
                   idx_ref, wgt_ref, pos_ref, cnt_ref, hs_ref, carry_ref):
    @pl.when(pl.program_id(0) == 0)
    def _():
        carry_ref[...] = jnp.zeros_like(carry_ref)

    x = x_ref[...]
    hsu = jnp.dot(x, wsh_ref[...], preferred_element_type=F32)
    f = hsu.shape[1] // 2
    hs_ref[...] = (_silu(hsu[:, :f]) * hsu[:, f:]).astype(hs_ref.dtype)

    logits = (lax.dot_general(whi_ref[...], x, _NT, preferred_element_type=F32)
              + lax.dot_general(wlo_ref[...], x, _NT, preferred_element_type=F32))
    scores = 1.0 / (1.0 + jnp.exp(-logits))
    sel = scores + bias_ref[...]
    n_e, tm = sel.shape
    epg = EXPERTS_PER_GROUP
    groups = [sel[g * epg:(g + 1) * epg, :] for g in range(N_EXPERT_GROUPS)]
    gscore = []
    for sg in groups:
        top1 = jnp.max(sg, axis=0, keepdims=True)
        is_top = sg == top1
        n_top = jnp.sum(is_top.astype(F32), axis=0, keepdims=True)
        second = jnp.max(jnp.where(is_top, -jnp.inf, sg), axis=0, keepdims=True)
        gscore.append(top1 + jnp.where(n_top >= 2.0, top1, second))
    masked = []
    for g in range(N_EXPERT_GROUPS):
        rank = jnp.zeros((1, tm), F32)
        for o in range(N_EXPERT_GROUPS):
            if o == g:
                continue
            beats = (gscore[o] >= gscore[g]) if o < g else (gscore[o] > gscore[g])
            rank = rank + beats.astype(F32)
        masked.append(jnp.where(rank < float(TOPK_GROUPS), groups[g], -jnp.inf))
    v = jnp.concatenate(masked, axis=0)
    row = lax.broadcasted_iota(jnp.int32, (n_e, tm), 0)
    chosen = jnp.zeros((n_e, tm), F32)
    picks, top_s = [], []
    for _ in range(TOP_K):
        best = jnp.max(v, axis=0, keepdims=True)
        first = jnp.min(jnp.where(v == best, row, n_e), axis=0, keepdims=True)
        pick = row == first
        picks.append(first)
        top_s.append(jnp.sum(jnp.where(pick, scores, 0.0), axis=0, keepdims=True))
        chosen = jnp.where(pick, 1.0, chosen)
        v = jnp.where(pick, -jnp.inf, v)
    total = top_s[0]
    for s in top_s[1:]:
        total = total + s
    running = jnp.dot(chosen.astype(BF16), tri_ref[...], preferred_element_type=F32) + carry_ref[...]
    pos = [jnp.sum(jnp.where(row == first, running - 1.0, 0.0), axis=0, keepdims=True) for first in picks]
    idx_ref[...] = jnp.concatenate(picks, axis=0)
    wgt_ref[...] = jnp.concatenate([s / total * ROUTED_SCALE for s in top_s], axis=0)
    pos_ref[...] = jnp.concatenate(pos, axis=0).astype(jnp.int32)
    carry_ref[...] += jnp.sum(chosen, axis=1, keepdims=True)
    cnt_ref[...] = carry_ref[...].astype(jnp.int32)


def _router(h, router_w, router_bias, sh_gate, sh_up, tm=1024):
    n, d = h.shape
    f = sh_gate.shape[1]
    wt = router_w.T
    w_hi = wt.astype(BF16)
    w_lo = (wt - w_hi.astype(F32)).astype(BF16)
    tri = (jnp.arange(tm)[:, None] <= jnp.arange(tm)[None, :]).astype(BF16)
    w_sh = jnp.concatenate([sh_gate, sh_up], axis=1).astype(BF16)
    fixed = lambda shape: pl.BlockSpec(shape, lambda i: (0, 0))
    per_tok = pl.BlockSpec((TOP_K, tm), lambda i: (0, i))
    return pl.pallas_call(
        _router_kernel,
        grid=(n // tm,),
        in_specs=[pl.BlockSpec((tm, d), lambda i: (i, 0)), fixed((N_EXPERTS, d)), fixed((N_EXPERTS, d)),
                  fixed((N_EXPERTS, 1)), fixed((tm, tm)), fixed((d, 2 * f))],
        out_specs=[per_tok, per_tok, per_tok, fixed((N_EXPERTS, 1)), pl.BlockSpec((tm, f), lambda i: (i, 0))],
        out_shape=[jax.ShapeDtypeStruct((TOP_K, n), jnp.int32), jax.ShapeDtypeStruct((TOP_K, n), F32),
                   jax.ShapeDtypeStruct((TOP_K, n), jnp.int32), jax.ShapeDtypeStruct((N_EXPERTS, 1), jnp.int32),
                   jax.ShapeDtypeStruct((n, f), BF16)],
        scratch_shapes=[pltpu.VMEM((N_EXPERTS, 1), F32)],
        compiler_params=_cparams(("arbitrary",), 48),
        name="router",
    )(h, w_hi, w_lo, router_bias.reshape(N_EXPERTS, 1), tri, w_sh)


def _start_record_gather(idx_ref, n, src_hbm, dst, sem, pitch, unroll=8):
    def body(j, carry):
        for u in range(unroll):
            r = j * unroll + u
            src_row = pl.multiple_of(idx_ref[0, 0, r] * REC_ROWS, REC_ROWS)
            dst_row = pl.multiple_of(r * pitch, 8)
            pltpu.make_async_copy(src_hbm.at[pl.ds(src_row, REC_ROWS), :], dst.at[pl.ds(dst_row, REC_ROWS), :], sem).start()
        return carry
    lax.fori_loop(0, n // unroll, body, 0)


def _wait_record_gather(n, src_hbm, dst, sem):
    pltpu.make_async_copy(src_hbm.at[pl.ds(0, n * REC_ROWS), :], dst.at[pl.ds(0, n * REC_ROWS), :], sem).wait()


def _moe_expert_kernel(te_ref, nt_ref, tok_ref, tok_next_ref, x_hbm, wg_ref, wu_ref, wd_ref, y_ref,
                       xbuf, sem, wg_bf, wu_bf, wd_bf):
    i = pl.program_id(0)
    n_tiles = nt_ref[0]
    slot = lax.rem(i, 2)
    tm = tok_ref.shape[2]
    half = D_MODEL // 2

    @pl.when(i == 0)
    def _():
        _start_record_gather(tok_ref, tm, x_hbm, xbuf.at[0], sem.at[0], GATHER_PITCH)

    @pl.when(i + 1 < n_tiles)
    def _():
        _start_record_gather(tok_next_ref, tm, x_hbm, xbuf.at[1 - slot], sem.at[1 - slot], GATHER_PITCH)

    @pl.when(i >= n_tiles)
    def _():
        y_ref[...] = jnp.zeros_like(y_ref)

    @pl.when(i < n_tiles)
    def _():
        @pl.when(jnp.logical_or(i == 0, te_ref[i] != te_ref[jnp.maximum(i - 1, 0)]))
        def _():
            wg_bf[...] = wg_ref[0].astype(BF16)
            wu_bf[...] = wu_ref[0].astype(BF16)
            wd_bf[...] = wd_ref[0].astype(BF16)

        _wait_record_gather(tm, x_hbm, xbuf.at[slot], sem.at[slot])
        lo, hi = _unpack_bf16_pairs(_load_records(xbuf.at[slot], 0, tm, GATHER_PITCH))
        lo = lo.astype(BF16)
        hi = hi.astype(BF16)
        pre_g = (jnp.dot(lo, wg_bf[:half, :], preferred_element_type=F32)
                 + jnp.dot(hi, wg_bf[half:, :], preferred_element_type=F32))
        pre_u = (jnp.dot(lo, wu_bf[:half, :], preferred_element_type=F32)
                 + jnp.dot(hi, wu_bf[half:, :], preferred_element_type=F32))
        hid = (_silu(pre_g) * pre_u).astype(BF16)
        chunk = 4 * LANES
        for c in range(half // chunk):
            y_lo = jnp.dot(hid, wd_bf[:, c * chunk:(c + 1) * chunk], preferred_element_type=F32)
            y_hi = jnp.dot(hid, wd_bf[:, half + c * chunk:half + (c + 1) * chunk], preferred_element_type=F32)
            words = _pack_bf16_pairs(y_lo, y_hi)
            for s in range(chunk // LANES):
                y_ref[pl.ds(c * (chunk // LANES) + s, tm, stride=REC_ROWS), :] = words[:, s * LANES:(s + 1) * LANES]


def _moe_experts(x_rec, tile_expert, n_tiles, tok_slot, w_gate, w_up, w_down):
    t_max, _, tm = tok_slot.shape
    e, d, f = w_gate.shape
    tok_spec = lambda shift: pl.BlockSpec(
        (1, 1, tm), functools.partial(lambda i, te, nt, shift: (jnp.minimum(i + shift, t_max - 1), 0, 0), shift=shift),
        memory_space=pltpu.SMEM)
    grid_spec = pltpu.PrefetchScalarGridSpec(
        num_scalar_prefetch=2,
        grid=(t_max,),
        in_specs=[tok_spec(0), tok_spec(1),
                  pl.BlockSpec(memory_space=pl.ANY),
                  pl.BlockSpec((1, d, f), lambda i, te, nt: (te[i], 0, 0)),
                  pl.BlockSpec((1, d, f), lambda i, te, nt: (te[i], 0, 0)),
                  pl.BlockSpec((1, f, d), lambda i, te, nt: (te[i], 0, 0))],
        out_specs=pl.BlockSpec((tm * REC_ROWS, LANES), lambda i, te, nt: (i, 0)),
        scratch_shapes=[pltpu.VMEM((2, tm * GATHER_PITCH, LANES), jnp.uint32), pltpu.SemaphoreType.DMA((2,)),
                        pltpu.VMEM((d, f), BF16), pltpu.VMEM((d, f), BF16), pltpu.VMEM((f, d), BF16)],
    )
    return pl.pallas_call(
        _moe_expert_kernel,
        grid_spec=grid_spec,
        out_shape=jax.ShapeDtypeStruct((t_max * tm * REC_ROWS, LANES), jnp.uint32),
        compiler_params=_cparams(("arbitrary",), 56),
        name="moe_experts",
    )(tile_expert, n_tiles, tok_slot, tok_slot, x_rec, w_gate, w_up, w_down)


def _moe_combine_kernel(*refs, has_next, n_steps):
    if has_next:
        (slot_ref, slot_next_ref, y_hbm, wgt_ref, hs_ref, wsd_ref, xres, gate, gpost, gnext, sh, sc,
         xout, hout, ybuf, sem) = refs
    else:
        slot_ref, slot_next_ref, y_hbm, wgt_ref, hs_ref, wsd_ref, xres, gate, gpost, xout, ybuf, sem = refs
    i = pl.program_id(0)
    buf = lax.rem(i, 2)
    tmc = xout.shape[0]
    n_rec = TOP_K * tmc

    @pl.when(i == 0)
    def _():
        _start_record_gather(slot_ref, n_rec, y_hbm, ybuf.at[0], sem.at[0], GATHER_PITCH)

    @pl.when(i + 1 < n_steps)
    def _():
        _start_record_gather(slot_next_ref, n_rec, y_hbm, ybuf.at[1 - buf], sem.at[1 - buf], GATHER_PITCH)

    _wait_record_gather(n_rec, y_hbm, ybuf.at[buf], sem.at[buf])
    wgt = wgt_ref[...]
    f_lo = f_hi = None
    for k in range(TOP_K):
        lo, hi = _unpack_bf16_pairs(_load_records(ybuf.at[buf], k * tmc * GATHER_PITCH, tmc, GATHER_PITCH))
        wk = wgt[:, k:k + 1]
        f_lo = wk * lo if f_lo is None else f_lo + wk * lo
        f_hi = wk * hi if f_hi is None else f_hi + wk * hi
    y = jnp.concatenate([f_lo, f_hi], axis=1) + jnp.dot(hs_ref[...], wsd_ref[...], preferred_element_type=F32)
    xn = xres[...] + gate[0] * ((y * _rms_scale(y)) * gpost[...])
    xout[...] = xn
    if has_next:
        hout[...] = (((xn * _rms_scale(xn)) * gnext[...]) * (1.0 + sc[0]) + sh[0]).astype(hout.dtype)


def _moe_combine(y_rec, slot_tiles, wgt, hs, w_sd, xres, gate, gpost, nxt):
    m, d = xres.shape
    n_steps, _, n_rec = slot_tiles.shape
    tmc = n_rec // TOP_K
    f = hs.shape[1]
    has_next = nxt is not None
    slot_spec = lambda shift: pl.BlockSpec(
        (1, 1, n_rec), functools.partial(lambda i, shift: (jnp.minimum(i + shift, n_steps - 1), 0, 0), shift=shift),
        memory_space=pltpu.SMEM)
    vec_spec = pl.BlockSpec((1, d), lambda i: (0, 0))
    mod_spec = pl.BlockSpec((1, 1, d), lambda i: (i * tmc // SEQ, 0, 0))
    row_spec = pl.BlockSpec((tmc, d), lambda i: (i, 0))
    in_specs = [slot_spec(0), slot_spec(1), pl.BlockSpec(memory_space=pl.ANY),
                pl.BlockSpec((tmc, TOP_K), lambda i: (i, 0)), pl.BlockSpec((tmc, f), lambda i: (i, 0)),
                pl.BlockSpec((f, d), lambda i: (0, 0)), row_spec, mod_spec, vec_spec]
    args = [slot_tiles, slot_tiles, y_rec, wgt, hs, w_sd, xres, gate, gpost.reshape(1, d)]
    out_specs = [row_spec]
    out_shape = [jax.ShapeDtypeStruct((m, d), F32)]
    if has_next:
        gnext, sh, sc = nxt
        in_specs += [vec_spec, mod_spec, mod_spec]
        args += [gnext.reshape(1, d), sh, sc]
        out_specs.append(row_spec)
        out_shape.append(jax.ShapeDtypeStruct((m, d), BF16))
    outs = pl.pallas_call(
        functools.partial(_moe_combine_kernel, has_next=has_next, n_steps=n_steps),
        grid=(n_steps,),
        in_specs=in_specs,
        out_specs=out_specs,
        out_shape=out_shape,
        scratch_shapes=[pltpu.VMEM((2, n_rec * GATHER_PITCH, LANES), jnp.uint32), pltpu.SemaphoreType.DMA((2,))],
        compiler_params=_cparams(("arbitrary",), 48),
        name="moe_combine",
    )(*args)
    return (outs[0], outs[1]) if has_next else (outs[0], None)


def _mod_parts(mods_layer, n_rows):
    return [mods_layer[:n_rows, j * D_MODEL:(j + 1) * D_MODEL].reshape(n_rows, 1, D_MODEL) for j in range(6)]


def _moe(h2, h2_rec, x1, gate2, gpost, nxt, router_w, router_bias, w_gate, w_up, w_down, sh_gate, sh_up, sh_down):
    n = h2.shape[0]
    tm, tmc = MOE_TILE, COMBINE_TOKENS
    t_max = n * TOP_K // tm + N_EXPERTS
    idx, wgt, pos, counts, hs = _router(h2, router_w, router_bias, sh_gate, sh_up)
    tiles = (counts[:, 0] + tm - 1) // tm
    tile_end = jnp.cumsum(tiles)
    n_tiles = tile_end[-1]
    first_slot = (tile_end - tiles) * tm
    slot = jnp.sum(jnp.where(idx[:, :, None] == jnp.arange(N_EXPERTS)[None, None, :], first_slot[None, None, :], 0),
                   axis=-1) + pos
    tok = jnp.broadcast_to(jnp.arange(n, dtype=jnp.int32)[None, :], slot.shape)
    tok_slot = jnp.zeros((t_max * tm,), jnp.int32).at[slot.reshape(-1)].set(tok.reshape(-1), unique_indices=True)
    tile_id = jnp.minimum(jnp.arange(t_max), n_tiles - 1)
    tile_expert = jnp.sum(tile_id[:, None] >= tile_end[None, :], axis=1).astype(jnp.int32)
    y_rec = _moe_experts(h2_rec, tile_expert, n_tiles.reshape(1).astype(jnp.int32), tok_slot.reshape(t_max, 1, tm),
                         w_gate, w_up, w_down)
    slot_tiles = slot.reshape(TOP_K, n // tmc, tmc).transpose(1, 0, 2).reshape(n // tmc, 1, TOP_K * tmc)
    return _moe_combine(y_rec, slot_tiles, wgt.T, hs, sh_down.astype(BF16), x1, gate2, gpost, nxt)


def kernel(x, c, ctx, c_ctx, ada_w, ada_b, norm_g, attn_w_in, attn_w_out, na_rpb, q_norm_g, k_norm_g, mix_w_in, mix_w_out, conv_w, sgu_w, sgu_b, sgu_norm_g, router_w, router_bias, moe_w_gate, moe_w_up, moe_w_down, shared_w_gate, shared_w_up, shared_w_down):
    nb, s, d = x.shape
    assert (s, d) == (SEQ, D_MODEL) and ctx.shape == (nb, CTX_LEN, d)
    x_lat = x.reshape(nb * s, d)
    x_ctx = ctx.reshape(nb * CTX_LEN, d)

    c_rows = jnp.concatenate([c, c_ctx[None, :], jnp.zeros((16 - nb - 1, d), F32)], axis=0)
    mods = _adaln(c_rows, ada_w, ada_b)
    lat_mods = [_mod_parts(mods[i], nb) for i in range(DEPTH)]
    ctx_mods = _mod_parts(mods[0, nb:nb + 1], 1)

    cos, sin = _rope_tables()
    bands, pats, na_idx = _na_static_tables()

    sh1, sc1 = lat_mods[0][0], lat_mods[0][1]
    h_lat = _norm_mod(x_lat, norm_g[0, 0], sh1, sc1, SEQ)

    for i in range(DEPTH):
        _, _, g1, sh2, sc2, g2 = lat_mods[i]
        if i % 2 == 0:
            e = i // 2
            w_in = attn_w_in[e]
            kv_lo, kv_hi = NA_DIM, 3 * NA_DIM
            gk_lo = 3 * NA_DIM + GQA_Q_DIM
            h_ctx = _norm_mod(x_ctx, norm_g[i, 0], ctx_mods[0], ctx_mods[1], nb * CTX_LEN)
            w_ctx = jnp.concatenate([w_in[:, kv_lo:kv_hi], w_in[:, gk_lo:]], axis=1).astype(BF16)
            p_lat = _matmul(h_lat, w_in.astype(BF16))
            p_ctx = _matmul(h_ctx, w_ctx)
            blk = 4 * HEAD_DIM
            qg = _qk_prep(p_lat, kv_hi // blk, GQA_Q_DIM // blk, q_norm_g[e], cos, sin, True)
            kg = _qk_prep(p_lat, gk_lo // blk, GQA_KV_DIM // blk, k_norm_g[e], cos, sin, True)
            kcg = _qk_prep(p_ctx, 2 * NA_DIM // blk, GQA_KV_DIM // blk, k_norm_g[e], cos, sin, False)
            bias = _na_bias_table(na_rpb[e], na_idx)
            o_na = _na_attention(p_lat, p_ctx, bias, bands, pats)
            o_gqa = _gqa_attention(qg, kg, p_lat, kcg, p_ctx,
                                   (gk_lo + GQA_KV_DIM) // HEAD_DIM, (2 * NA_DIM + GQA_KV_DIM) // HEAD_DIM)
            parts = [o_na, o_gqa]
            w_out = attn_w_out[e].astype(BF16)
        else:
            o = i // 2
            p_mix = _matmul(h_lat, mix_w_in[o].astype(BF16))
            parts = [_conv_sgu(p_mix, conv_w[o], sgu_w[o], sgu_b[o], sgu_norm_g[o])]
            wo = mix_w_out[o]
            w_out = jnp.concatenate(
                [wo[:CONV_DIM].reshape(SGU_GROUPS, SGU_CH, d), wo[CONV_DIM:].reshape(SGU_GROUPS, SGU_CH, d)],
                axis=1).reshape(CONV_DIM + SGU_DIM, d).astype(BF16)
        x1, h2, h2_rec = _matmul_post(parts, w_out, x_lat, g1, norm_g[i, 1], norm_g[i, 2], sh2, sc2)
        nxt = None
        if i + 1 < DEPTH:
            nxt = (norm_g[i + 1, 0], lat_mods[i + 1][0], lat_mods[i + 1][1])
        x_lat, h_lat = _moe(h2, h2_rec, x1, g2, norm_g[i, 3], nxt, router_w[i], router_bias[i],
                            moe_w_gate[i], moe_w_up[i], moe_w_down[i],
                            shared_w_gate[i], shared_w_up[i], shared_w_down[i])
    return x_lat.reshape(nb, s, d)
```

```python
import functools

import numpy as np
import jax
import jax.numpy as jnp
from jax import lax
from jax.experimental import pallas as pl
from jax.experimental.pallas import tpu as pltpu

F32 = jnp.float32
BF16 = jnp.bfloat16

D_MODEL = 4096
SEQ = 2048
DEPTH = 2
GRID_W = 64
GRID_ROWS = SEQ // GRID_W
CTX_LEN = 256
HEAD_DIM = 128
N_NA_HEADS = 16
N_GQA_HEADS = 16
N_GQA_KV = 4
GQA_REP = N_GQA_HEADS // N_GQA_KV
NA_KH = 8
NA_KW = 16
NA_BAND_ROWS = NA_KH + 1
NA_KEYS = NA_BAND_ROWS * GRID_W
NA_QBLK = 2 * GRID_W
NA_NBLK = GRID_ROWS // 2
ROPE_THETA = 10000.0
CONV_DIM = D_MODEL // 2
SGU_DIM = D_MODEL // 2
SGU_GROUPS = 8
SGU_CH = SGU_DIM // SGU_GROUPS
SGU_CHUNK = 128
N_EXPERTS = 64
N_EXPERT_GROUPS = 8
EXPERTS_PER_GROUP = N_EXPERTS // N_EXPERT_GROUPS
TOPK_GROUPS = 4
TOP_K = 8
D_EXPERT = D_MODEL // 16
ROUTED_SCALE = 2.5
EPS = 1e-6
NEG = -1e30
NA_DIM = N_NA_HEADS * HEAD_DIM
GQA_Q_DIM = N_GQA_HEADS * HEAD_DIM
GQA_KV_DIM = N_GQA_KV * HEAD_DIM
ATTN_SCALE = HEAD_DIM ** -0.5
LANES = 128
REC_ROWS = D_MODEL // 2 // LANES
MOE_TILE = 256
GATHER_PITCH = 24
COMBINE_TOKENS = 64

V7X_VMEM_BYTES = 64 * 1024 * 1024
MIB = 1024 * 1024


def _cparams(semantics, vmem_mib):
    assert vmem_mib * MIB < V7X_VMEM_BYTES
    return pltpu.CompilerParams(dimension_semantics=semantics, vmem_limit_bytes=vmem_mib * MIB)


def _rms_scale(x):
    return lax.rsqrt(jnp.mean(x * x, axis=-1, keepdims=True) + EPS)


def _silu(x):
    return x / (1.0 + jnp.exp(-x))


def _adaln_kernel(c_ref, w_ref, b_ref, o_ref):
    a = _silu(c_ref[...])
    o_ref[0] = jnp.dot(a.astype(BF16), w_ref[0].astype(BF16), preferred_element_type=F32) + b_ref[0]


def _adaln(c_rows, ada_w, ada_b):
    depth, d, n6 = ada_w.shape
    rows = c_rows.shape[0]
    tn = 512
    return pl.pallas_call(
        _adaln_kernel,
        grid=(depth, n6 // tn),
        in_specs=[pl.BlockSpec((rows, d), lambda l, j: (0, 0)),
                  pl.BlockSpec((1, d, tn), lambda l, j: (l, 0, j)),
                  pl.BlockSpec((1, 1, tn), lambda l, j: (l, 0, j))],
        out_specs=pl.BlockSpec((1, rows, tn), lambda l, j: (l, 0, j)),
        out_shape=jax.ShapeDtypeStruct((depth, rows, n6), F32),
        compiler_params=_cparams(("parallel", "parallel"), 40),
        name="adaln",
    )(c_rows, ada_w, ada_b.reshape(depth, 1, n6))


def _norm_mod_kernel(x_ref, g_ref, sh_ref, sc_ref, o_ref):
    x = x_ref[...]
    y = (x * _rms_scale(x)) * g_ref[...]
    o_ref[...] = (y * (1.0 + sc_ref[0]) + sh_ref[0]).astype(o_ref.dtype)


def _norm_mod(x, g, sh, sc, rows_per_mod):
    m, d = x.shape
    tm = 256
    mod_spec = pl.BlockSpec((1, 1, d), lambda i: (i * tm // rows_per_mod, 0, 0))
    return pl.pallas_call(
        _norm_mod_kernel,
        grid=(m // tm,),
        in_specs=[pl.BlockSpec((tm, d), lambda i: (i, 0)),
                  pl.BlockSpec((1, d), lambda i: (0, 0)),
                  mod_spec, mod_spec],
        out_specs=pl.BlockSpec((tm, d), lambda i: (i, 0)),
        out_shape=jax.ShapeDtypeStruct((m, d), BF16),
        compiler_params=_cparams(("parallel",), 40),
        name="norm_mod",
    )(x, g.reshape(1, d), sh, sc)


def _mm_kernel(x_ref, w_ref, o_ref):
    o_ref[...] = jnp.dot(x_ref[...], w_ref[...], preferred_element_type=F32).astype(o_ref.dtype)


def _matmul(x, w, tm=1024, tn=512):
    m, k = x.shape
    n = w.shape[1]
    tm = min(tm, m)
    return pl.pallas_call(
        _mm_kernel,
        grid=(m // tm, n // tn),
        in_specs=[pl.BlockSpec((tm, k), lambda i, j: (i, 0)),
                  pl.BlockSpec((k, tn), lambda i, j: (0, j))],
        out_specs=pl.BlockSpec((tm, tn), lambda i, j: (i, j)),
        out_shape=jax.ShapeDtypeStruct((m, n), BF16),
        compiler_params=_cparams(("parallel", "arbitrary"), 48),
        name="matmul",
    )(x, w)


def _pack_bf16_pairs(lo, hi):
    return pltpu.pack_elementwise([lo, hi], packed_dtype=BF16)


def _unpack_bf16_pairs(words):
    lo = pltpu.unpack_elementwise(words, index=0, packed_dtype=BF16, unpacked_dtype=F32)
    hi = pltpu.unpack_elementwise(words, index=1, packed_dtype=BF16, unpacked_dtype=F32)
    return lo, hi


def _store_records(ref, first_row, words, pitch):
    for s in range(REC_ROWS):
        ref[pl.ds(first_row + s, words.shape[0], stride=pitch), :] = words[:, s * 128:(s + 1) * 128]


def _load_records(ref, first_row, n, pitch):
    return jnp.concatenate([ref[pl.ds(first_row + s, n, stride=pitch), :] for s in range(REC_ROWS)], axis=1)


def _mm_post_kernel(*refs, n_parts, nk, n_chunks, row_chunk):
    lhs = refs[:n_parts]
    ws = refs[n_parts:2 * n_parts]
    xres, gate, gpost, gnext, sh, sc, xout, hout, hrec, vecs = refs[2 * n_parts:]
    k = pl.program_id(1)
    tm, d = xout.shape
    tn = d // n_chunks

    def part(c):
        acc = None
        for l, w in zip(lhs, ws):
            t = jnp.dot(l[...], w[:, c * tn:(c + 1) * tn], preferred_element_type=F32)
            acc = t if acc is None else acc + t
        return acc

    @pl.when(k == 0)
    def _():
        for c in range(n_chunks):
            xout[:, c * tn:(c + 1) * tn] = part(c)

    @pl.when(k > 0)
    def _():
        for c in range(n_chunks):
            xout[:, c * tn:(c + 1) * tn] += part(c)

    @pl.when(k == nk - 1)
    def _():
        vecs[0:1, :] = gate[0] * gpost[...]
        vecs[1:2, :] = gnext[...] * (1.0 + sc[0])

        def body(r, carry):
            rows = pl.ds(pl.multiple_of(r * row_chunk, row_chunk), row_chunk)
            y = xout[rows, :]
            xn = xres[rows, :] + (y * _rms_scale(y)) * vecs[0:1, :]
            xout[rows, :] = xn
            h = (xn * _rms_scale(xn)) * vecs[1:2, :] + sh[0]
            hout[rows, :] = h.astype(hout.dtype)
            half = d // 2
            first = pl.multiple_of(r * (row_chunk * REC_ROWS), row_chunk * REC_ROWS)
            _store_records(hrec, first, _pack_bf16_pairs(h[:, :half], h[:, half:]), REC_ROWS)
            return carry
        lax.fori_loop(0, tm // row_chunk, body, 0)


def _matmul_post(lhs_parts, w, xres, gate, gpost, gnext, sh, sc, tm=512):
    m, d = xres.shape
    n_parts = len(lhs_parts)
    kp = lhs_parts[0].shape[1]
    assert all(l.shape == (m, kp) for l in lhs_parts) and w.shape == (n_parts * kp, d)
    tk = 512 // n_parts
    nk = kp // tk
    vec_spec = pl.BlockSpec((1, d), lambda i, k: (0, 0))
    mod_spec = pl.BlockSpec((1, 1, d), lambda i, k: (i * tm // SEQ, 0, 0))
    row_spec = pl.BlockSpec((tm, d), lambda i, k: (i, 0))
    in_specs = [pl.BlockSpec((tm, tk), lambda i, k: (i, k)) for _ in range(n_parts)]
    in_specs += [pl.BlockSpec((tk, d), functools.partial(lambda i, k, off: (k + off, 0), off=p * nk))
                 for p in range(n_parts)]
    in_specs += [pl.BlockSpec((tm, d), lambda i, k: (i, 0), pipeline_mode=pl.Buffered(1)),
                 mod_spec, vec_spec, vec_spec, mod_spec, mod_spec]
    args = list(lhs_parts) + [w] * n_parts + [xres, gate, gpost.reshape(1, d), gnext.reshape(1, d), sh, sc]
    return pl.pallas_call(
        functools.partial(_mm_post_kernel, n_parts=n_parts, nk=nk, n_chunks=4, row_chunk=32),
        grid=(m // tm, nk),
        in_specs=in_specs,
        out_specs=[row_spec, row_spec, pl.BlockSpec((tm * REC_ROWS, 128), lambda i, k: (i, 0))],
        out_shape=[jax.ShapeDtypeStruct((m, d), F32), jax.ShapeDtypeStruct((m, d), BF16),
                   jax.ShapeDtypeStruct((m * REC_ROWS, 128), jnp.int32)],
        scratch_shapes=[pltpu.VMEM((8, d), F32)],
        compiler_params=_cparams(("parallel", "arbitrary"), 56),
        name="matmul_post",
    )(*args)


def _qk_prep_kernel(x_ref, g_ref, cos_ref, sin_ref, o_ref, *, rope):
    n_heads = x_ref.shape[1] // HEAD_DIM
    for hh in range(n_heads):
        cols = slice(hh * HEAD_DIM, (hh + 1) * HEAD_DIM)
        x = x_ref[:, cols].astype(F32)
        y = (x * _rms_scale(x)) * g_ref[...]
        if rope:
            lane = lax.broadcasted_iota(jnp.int32, y.shape, 1)
            quarter = HEAD_DIM // 4
            partner = jnp.where(jnp.bitwise_and(lane, 2 * quarter - 1) < quarter,
                                pltpu.roll(y, HEAD_DIM - quarter, axis=1), pltpu.roll(y, quarter, axis=1))
            y = y * cos_ref[...] + partner * sin_ref[...]
        o_ref[:, cols] = y.astype(o_ref.dtype)


def _qk_prep(p, col_block0, n_col_blocks, g, cos, sin, rope):
    m = p.shape[0]
    tm, tc = 512, 4 * HEAD_DIM
    pos_blocks = SEQ // tm
    return pl.pallas_call(
        functools.partial(_qk_prep_kernel, rope=rope),
        grid=(m // tm, n_col_blocks),
        in_specs=[pl.BlockSpec((tm, tc), lambda i, j: (i, col_block0 + j)),
                  pl.BlockSpec((1, HEAD_DIM), lambda i, j: (0, 0)),
                  pl.BlockSpec((tm, HEAD_DIM), lambda i, j: (i % pos_blocks, 0)),
                  pl.BlockSpec((tm, HEAD_DIM), lambda i, j: (i % pos_blocks, 0))],
        out_specs=pl.BlockSpec((tm, tc), lambda i, j: (i, j)),
        out_shape=jax.ShapeDtypeStruct((m, tc * n_col_blocks), BF16),
        compiler_params=_cparams(("parallel", "parallel"), 32),
        name="qk_prep",
    )(p, g.reshape(1, HEAD_DIM), cos, sin)


def _rope_tables():
    half = HEAD_DIM // 4
    t = jnp.arange(SEQ)
    freqs = ROPE_THETA ** (-jnp.arange(half, dtype=F32) / half)
    ang_r = (t // GRID_W).astype(F32)[:, None] * freqs[None, :]
    ang_c = (t % GRID_W).astype(F32)[:, None] * freqs[None, :]
    cos = jnp.concatenate([jnp.cos(ang_r)] * 2 + [jnp.cos(ang_c)] * 2, axis=-1)
    sin = jnp.concatenate([-jnp.sin(ang_r), jnp.sin(ang_r), -jnp.sin(ang_c), jnp.sin(ang_c)], axis=-1)
    return cos, sin


def _na_static_tables():
    rows, kh, kw = GRID_ROWS, min(NA_KH, GRID_ROWS), NA_KW
    nb_rows = min(kh + 1, rows)
    assert nb_rows == NA_BAND_ROWS
    col = np.arange(GRID_W)
    col_start = np.clip(col - kw // 2, 0, GRID_W - kw)
    patterns, pat_of_block, bands = [], [], []
    for b in range(NA_NBLK):
        q_rows = 2 * b + np.arange(2)
        row_start = np.clip(q_rows - kh // 2, 0, rows - kh)
        band = int(np.clip(row_start[0], 0, rows - nb_rows))
        key = (tuple(q_rows - band), tuple(row_start - band))
        if key not in patterns:
            patterns.append(key)
        pat_of_block.append(patterns.index(key))
        bands.append(band)
    idx = np.full((len(patterns), NA_QBLK, NA_KEYS), -1, np.int32)
    for p, (q_rel, rs_rel) in enumerate(patterns):
        qr = np.repeat(np.array(q_rel), GRID_W)
        qc = np.tile(col, 2)
        rs = np.repeat(np.array(rs_rel), GRID_W)
        cs = np.tile(col_start, 2)
        kr = np.repeat(np.arange(nb_rows), GRID_W)
        kc = np.tile(col, nb_rows)
        in_win = ((kr[None, :] >= rs[:, None]) & (kr[None, :] < rs[:, None] + kh)
                  & (kc[None, :] >= cs[:, None]) & (kc[None, :] < cs[:, None] + kw))
        dr = np.clip(kr[None, :] - qr[:, None] + NA_KH - 1, 0, 2 * NA_KH - 2)
        dc = np.clip(kc[None, :] - qc[:, None] + NA_KW - 1, 0, 2 * NA_KW - 2)
        idx[p] = np.where(in_win, dr * (2 * NA_KW - 1) + dc, -1)
    return np.array(bands, np.int32), np.array(pat_of_block, np.int32), idx


def _na_bias_kernel(idx_ref, rpb_ref, o_ref):
    idx = idx_ref[0]
    n_tab = rpb_ref.shape[1]
    onehot = (lax.broadcasted_iota(jnp.int32, (n_tab, idx.shape[1]), 0) == idx).astype(BF16)
    r = rpb_ref[...]
    hi = r.astype(BF16)
    r1 = r - hi.astype(F32)
    mid = r1.astype(BF16)
    lo = (r1 - mid.astype(F32)).astype(BF16)
    val = (jnp.dot(hi, onehot, preferred_element_type=F32) + jnp.dot(mid, onehot, preferred_element_type=F32)
           + jnp.dot(lo, onehot, preferred_element_type=F32))
    o_ref[0] = jnp.where(idx < 0, NEG, val)


def _na_bias_table(rpb, idx):
    n_pat = idx.shape[0]
    h = rpb.shape[0]
    n_tab = 512
    flat = NA_QBLK * NA_KEYS
    tile = 8 * NA_KEYS
    rpb_flat = jnp.pad(rpb.reshape(h, -1), ((0, 0), (0, n_tab - rpb.shape[1] * rpb.shape[2])))
    out = pl.pallas_call(
        _na_bias_kernel,
        grid=(n_pat, flat // tile),
        in_specs=[pl.BlockSpec((1, 1, tile), lambda p, j: (p, 0, j)),
                  pl.BlockSpec((h, n_tab), lambda p, j: (0, 0))],
        out_specs=pl.BlockSpec((1, h, tile), lambda p, j: (p, 0, j)),
        out_shape=jax.ShapeDtypeStruct((n_pat, h, flat), F32),
        compiler_params=_cparams(("parallel", "parallel"), 32),
        name="na_bias",
    )(jnp.asarray(idx.reshape(n_pat, 1, flat)), rpb_flat)
    return out.reshape(n_pat, h, NA_QBLK, NA_KEYS)


_NT = (((1,), (1,)), ((), ()))


def _na_attn_kernel(q_ref, k_ref, v_ref, kc_ref, vc_ref, bias_ref, o_ref, *, bands, pats):
    kc = kc_ref[...]
    vc = vc_ref[...]

    for n in range(NA_NBLK):
        q_rows = slice(n * NA_QBLK, (n + 1) * NA_QBLK)
        key_rows = slice(bands[n] * GRID_W, bands[n] * GRID_W + NA_KEYS)
        q = q_ref[q_rows, :]
        bias = bias_ref[pats[n], 0]
        s = lax.dot_general(q, k_ref[key_rows, :], _NT, preferred_element_type=F32) * ATTN_SCALE + bias
        s = jnp.where(bias <= 0.5 * NEG, NEG, s)
        s_ctx = lax.dot_general(q, kc, _NT, preferred_element_type=F32) * ATTN_SCALE
        m = jnp.maximum(jnp.max(s, axis=-1, keepdims=True), jnp.max(s_ctx, axis=-1, keepdims=True))
        p = jnp.exp(s - m)
        p_ctx = jnp.exp(s_ctx - m)
        denom = jnp.sum(p, axis=-1, keepdims=True) + jnp.sum(p_ctx, axis=-1, keepdims=True)
        o = (jnp.dot(p.astype(BF16), v_ref[key_rows, :], preferred_element_type=F32)
             + jnp.dot(p_ctx.astype(BF16), vc, preferred_element_type=F32))
        o_ref[q_rows, :] = (o / denom).astype(o_ref.dtype)


def _na_attention(p_lat, p_ctx, bias, bands, pats):
    n_tok = p_lat.shape[0]
    nb = n_tok // SEQ
    n_pat = bias.shape[0]
    hd = HEAD_DIM
    return pl.pallas_call(
        functools.partial(_na_attn_kernel, bands=tuple(int(b) for b in bands), pats=tuple(int(p) for p in pats)),
        grid=(N_NA_HEADS, nb),
        in_specs=[pl.BlockSpec((SEQ, hd), lambda h, b: (b, h)),
                  pl.BlockSpec((SEQ, hd), lambda h, b: (b, N_NA_HEADS + h)),
                  pl.BlockSpec((SEQ, hd), lambda h, b: (b, 2 * N_NA_HEADS + h)),
                  pl.BlockSpec((CTX_LEN, hd), lambda h, b: (b, h)),
                  pl.BlockSpec((CTX_LEN, hd), lambda h, b: (b, N_NA_HEADS + h)),
                  pl.BlockSpec((n_pat, 1, NA_QBLK, NA_KEYS), lambda h, b: (0, h, 0, 0))],
        out_specs=pl.BlockSpec((SEQ, hd), lambda h, b: (b, h)),
        out_shape=jax.ShapeDtypeStruct((n_tok, NA_DIM), BF16),
        compiler_params=_cparams(("parallel", "parallel"), 32),
        name="na_attention",
    )(p_lat, p_lat, p_lat, p_ctx, p_ctx, bias)


def _gqa_attn_kernel(q_ref, k_ref, v_ref, kc_ref, vc_ref, o_ref):
    k = k_ref[...]
    v = v_ref[...]
    kc = kc_ref[...]
    vc = vc_ref[...]
    for r in range(GQA_REP):
        cols = slice(r * HEAD_DIM, (r + 1) * HEAD_DIM)
        q = q_ref[:, cols]
        s = lax.dot_general(q, k, _NT, preferred_element_type=F32) * ATTN_SCALE
        s_ctx = lax.dot_general(q, kc, _NT, preferred_element_type=F32) * ATTN_SCALE
        m = jnp.maximum(jnp.max(s, axis=-1, keepdims=True), jnp.max(s_ctx, axis=-1, keepdims=True))
        p = jnp.exp(s - m)
        p_ctx = jnp.exp(s_ctx - m)
        denom = jnp.sum(p, axis=-1, keepdims=True) + jnp.sum(p_ctx, axis=-1, keepdims=True)
        o = (jnp.dot(p.astype(BF16), v, preferred_element_type=F32)
             + jnp.dot(p_ctx.astype(BF16), vc, preferred_element_type=F32))
        o_ref[:, cols] = (o / denom).astype(o_ref.dtype)


def _gqa_attention(qg, kg, p_lat, kcg, p_ctx, v_col_block, vc_col_block, tq=256):
    n_tok = qg.shape[0]
    nb = n_tok // SEQ
    hd = HEAD_DIM
    qt = SEQ // tq
    return pl.pallas_call(
        _gqa_attn_kernel,
        grid=(nb, N_GQA_KV, qt),
        in_specs=[pl.BlockSpec((tq, GQA_REP * hd), lambda b, g, t: (b * qt + t, g)),
                  pl.BlockSpec((SEQ, hd), lambda b, g, t: (b, g)),
                  pl.BlockSpec((SEQ, hd), lambda b, g, t: (b, v_col_block + g)),
                  pl.BlockSpec((CTX_LEN, hd), lambda b, g, t: (b, g)),
                  pl.BlockSpec((CTX_LEN, hd), lambda b, g, t: (b, vc_col_block + g))],
        out_specs=pl.BlockSpec((tq, GQA_REP * hd), lambda b, g, t: (b * qt + t, g)),
        out_shape=jax.ShapeDtypeStruct((n_tok, GQA_Q_DIM), BF16),
        compiler_params=_cparams(("parallel", "parallel", "parallel"), 48),
        name="gqa_attention",
    )(qg, kg, p_lat, kcg, p_ctx)


def _mixer_kernel(gb_ref, gc_ref, xc_ref, u_ref, v_ref, cw_ref, sw_ref, sb_ref, sg_ref, o_ref):
    s, c = gb_ref.shape
    z = gc_ref[...].astype(F32) * xc_ref[...].astype(F32)
    row = lax.broadcasted_iota(jnp.int32, (s, c), 0)
    z_prev = jnp.where(row == 0, 0.0, pltpu.roll(z, 1, axis=0))
    z_next = jnp.where(row == s - 1, 0.0, pltpu.roll(z, s - 1, axis=0))
    cw = cw_ref[...]
    conv = z_prev * cw[0:1, :] + z * cw[1:2, :] + z_next * cw[2:3, :]
    o_ref[:, :c] = (gb_ref[...].astype(F32) * conv).astype(o_ref.dtype)
    w = sw_ref[0].astype(BF16)
    for n in range(s // SGU_CHUNK):
        rows = slice(n * SGU_CHUNK, (n + 1) * SGU_CHUNK)
        vv = v_ref[rows, :].astype(F32)
        vg = (vv * _rms_scale(vv)) * sg_ref[0]
        mix = jnp.dot(w, vg.astype(BF16), preferred_element_type=F32) + sb_ref[0]
        o_ref[rows, c:] = (u_ref[rows, :].astype(F32) * mix).astype(o_ref.dtype)


def _conv_sgu(p, conv_w, sgu_w, sgu_b, sgu_g):
    n_tok = p.shape[0]
    nb = n_tok // SEQ
    c = SGU_CH
    blocks = CONV_DIM // c
    col = lambda off: pl.BlockSpec((SEQ, c), functools.partial(lambda b, g, off: (b, off + g), off=off))
    return pl.pallas_call(
        _mixer_kernel,
        grid=(nb, SGU_GROUPS),
        in_specs=[col(0), col(blocks), col(2 * blocks), col(3 * blocks), col(4 * blocks),
                  pl.BlockSpec((3, c), lambda b, g: (0, g)),
                  pl.BlockSpec((1, SGU_CHUNK, SGU_CHUNK), lambda b, g: (g, 0, 0)),
                  pl.BlockSpec((1, SGU_CHUNK, 1), lambda b, g: (g, 0, 0)),
                  pl.BlockSpec((1, 1, c), lambda b, g: (g, 0, 0))],
        out_specs=pl.BlockSpec((SEQ, 2 * c), lambda b, g: (b, g)),
        out_shape=jax.ShapeDtypeStruct((n_tok, CONV_DIM + SGU_DIM), BF16),
        compiler_params=_cparams(("parallel", "parallel"), 48),
        name="conv_sgu",
    )(p, p, p, p, p, conv_w, sgu_w, sgu_b.reshape(SGU_GROUPS, SGU_CHUNK, 1), sgu_g.reshape(SGU_GROUPS, 1, c))


def _router_kernel(x_ref, whi_ref, wlo_ref, bias_ref, tri_ref, wsh_ref,
                   idx_ref, wgt_ref, pos_ref, cnt_ref, hs_ref, carry_ref):
    @pl.when(pl.program_id(0) == 0)
    def _():
        carry_ref[...] = jnp.zeros_like(carry_ref)

    x = x_ref[...]
    hsu = jnp.dot(x, wsh_ref[...], preferred_element_type=F32)
    f = hsu.shape[1] // 2
    hs_ref[...] = (_silu(hsu[:, :f]) * hsu[:, f:]).astype(hs_ref.dtype)

    logits = (lax.dot_general(whi_ref[...], x, _NT, preferred_element_type=F32)
              + lax.dot_general(wlo_ref[...], x, _NT, preferred_element_type=F32))
    scores = 1.0 / (1.0 + jnp.exp(-logits))
    sel = scores + bias_ref[...]
    n_e, tm = sel.shape
    epg = EXPERTS_PER_GROUP
    groups = [sel[g * epg:(g + 1) * epg, :] for g in range(N_EXPERT_GROUPS)]
    gscore = []
    for sg in groups:
        top1 = jnp.max(sg, axis=0, keepdims=True)
        is_top = sg == top1
        n_top = jnp.sum(is_top.astype(F32), axis=0, keepdims=True)
        second = jnp.max(jnp.where(is_top, -jnp.inf, sg), axis=0, keepdims=True)
        gscore.append(top1 + jnp.where(n_top >= 2.0, top1, second))
    masked = []
    for g in range(N_EXPERT_GROUPS):
        rank = jnp.zeros((1, tm), F32)
        for o in range(N_EXPERT_GROUPS):
            if o == g:
                continue
            beats = (gscore[o] >= gscore[g]) if o < g else (gscore[o] > gscore[g])
            rank = rank + beats.astype(F32)
        masked.append(jnp.where(rank < float(TOPK_GROUPS), groups[g], -jnp.inf))
    v = jnp.concatenate(masked, axis=0)
    row = lax.broadcasted_iota(jnp.int32, (n_e, tm), 0)
    chosen = jnp.zeros((n_e, tm), F32)
    picks, top_s = [], []
    for _ in range(TOP_K):
        best = jnp.max(v, axis=0, keepdims=True)
        first = jnp.min(jnp.where(v == best, row, n_e), axis=0, keepdims=True)
        pick = row == first
        picks.append(first)
        top_s.append(jnp.sum(jnp.where(pick, scores, 0.0), axis=0, keepdims=True))
        chosen = jnp.where(pick, 1.0, chosen)
        v = jnp.where(pick, -jnp.inf, v)
    total = top_s[0]
    for s in top_s[1:]:
        total = total + s
    running = jnp.dot(chosen.astype(BF16), tri_ref[...], preferred_element_type=F32) + carry_ref[...]
    pos = [jnp.sum(jnp.where(row == first, running - 1.0, 0.0), axis=0, keepdims=True) for first in picks]
    idx_ref[...] = jnp.concatenate(picks, axis=0)
    wgt_ref[...] = jnp.concatenate([s / total * ROUTED_SCALE for s in top_s], axis=0)
    pos_ref[...] = jnp.concatenate(pos, axis=0).astype(jnp.int32)
    carry_ref[...] += jnp.sum(chosen, axis=1, keepdims=True)
    cnt_ref[...] = carry_ref[...].astype(jnp.int32)


def _router(h, router_w, router_bias, sh_gate, sh_up, tm=1024):
    n, d = h.shape
    f = sh_gate.shape[1]
    wt = router_w.T
    w_hi = wt.astype(BF16)
    w_lo = (wt - w_hi.astype(F32)).astype(BF16)
    tri = (jnp.arange(tm)[:, None] <= jnp.arange(tm)[None, :]).astype(BF16)
    w_sh = jnp.concatenate([sh_gate, sh_up], axis=1).astype(BF16)
    fixed = lambda shape: pl.BlockSpec(shape, lambda i: (0, 0))
    per_tok = pl.BlockSpec((TOP_K, tm), lambda i: (0, i))
    return pl.pallas_call(
        _router_kernel,
        grid=(n // tm,),
        in_specs=[pl.BlockSpec((tm, d), lambda i: (i, 0)), fixed((N_EXPERTS, d)), fixed((N_EXPERTS, d)),
                  fixed((N_EXPERTS, 1)), fixed((tm, tm)), fixed((d, 2 * f))],
        out_specs=[per_tok, per_tok, per_tok, fixed((N_EXPERTS, 1)), pl.BlockSpec((tm, f), lambda i: (i, 0))],
        out_shape=[jax.ShapeDtypeStruct((TOP_K, n), jnp.int32), jax.ShapeDtypeStruct((TOP_K, n), F32),
                   jax.ShapeDtypeStruct((TOP_K, n), jnp.int32), jax.ShapeDtypeStruct((N_EXPERTS, 1), jnp.int32),
                   jax.ShapeDtypeStruct((n, f), BF16)],
        scratch_shapes=[pltpu.VMEM((N_EXPERTS, 1), F32)],
        compiler_params=_cparams(("arbitrary",), 48),
        name="router",
    )(h, w_hi, w_lo, router_bias.reshape(N_EXPERTS, 1), tri, w_sh)


def _start_record_gather(idx_ref, n, src_hbm, dst, sem, pitch, unroll=8):
    def body(j, carry):
        for u in range(unroll):
            r = j * unroll + u
            src_row = pl.multiple_of(idx_ref[0, 0, r] * REC_ROWS, REC_ROWS)
            dst_row = pl.multiple_of(r * pitch, 8)
            pltpu.make_async_copy(src_hbm.at[pl.ds(src_row, REC_ROWS), :], dst.at[pl.ds(dst_row, REC_ROWS), :], sem).start()
        return carry
    lax.fori_loop(0, n // unroll, body, 0)


def _wait_record_gather(n, src_hbm, dst, sem):
    pltpu.make_async_copy(src_hbm.at[pl.ds(0, n * REC_ROWS), :], dst.at[pl.ds(0, n * REC_ROWS), :], sem).wait()


def _issue_records(idx_ref, r0, r1, src_hbm, dst, sem, pitch):
    for r in range(r0, r1):
        src_row = pl.multiple_of(idx_ref[0, 0, r] * REC_ROWS, REC_ROWS)
        pltpu.make_async_copy(src_hbm.at[pl.ds(src_row, REC_ROWS), :], dst.at[pl.ds(r * pitch, REC_ROWS), :], sem).start()


def _moe_expert_kernel(te_ref, tok_ref, tok_next_ref, x_hbm, wg_ref, wu_ref, wd_ref, y_ref,
                       xbuf, sem, wg_bf, wu_bf, wd_bf):
    i = pl.program_id(0)
    last = pl.num_programs(0) - 1
    slot = lax.rem(i, 2)
    tm = tok_ref.shape[2]
    half = D_MODEL // 2
    chunk = 4 * LANES
    n_chunks = half // chunk

    @pl.when(i == 0)
    def _():
        _start_record_gather(tok_ref, tm, x_hbm, xbuf.at[0], sem.at[0], GATHER_PITCH)

    @pl.when(jnp.logical_or(i == 0, te_ref[i] != te_ref[jnp.maximum(i - 1, 0)]))
    def _():
        wg_bf[...] = wg_ref[0, 0].astype(BF16)
        wu_bf[...] = wu_ref[0, 0].astype(BF16)
        wd_bf[...] = wd_ref[0, 0].astype(BF16)

    _wait_record_gather(tm, x_hbm, xbuf.at[slot], sem.at[slot])
    lo, hi = _unpack_bf16_pairs(_load_records(xbuf.at[slot], 0, tm, GATHER_PITCH))
    lo = lo.astype(BF16)
    hi = hi.astype(BF16)
    n_groups = n_chunks + 2
    bounds = [tm * g // n_groups for g in range(n_groups + 1)]
    prefetch = lambda g: _issue_records(tok_next_ref, bounds[g], bounds[g + 1], x_hbm, xbuf.at[1 - slot],
                                        sem.at[1 - slot], GATHER_PITCH)
    prefetch(0)
    pre_g = (jnp.dot(lo, wg_bf[:half, :], preferred_element_type=F32)
             + jnp.dot(hi, wg_bf[half:, :], preferred_element_type=F32))
    prefetch(1)
    pre_u = (jnp.dot(lo, wu_bf[:half, :], preferred_element_type=F32)
             + jnp.dot(hi, wu_bf[half:, :], preferred_element_type=F32))
    hid = (_silu(pre_g) * pre_u).astype(BF16)
    for c in range(n_chunks):
        prefetch(2 + c)
        y_lo = jnp.dot(hid, wd_bf[:, c * chunk:(c + 1) * chunk], preferred_element_type=F32)
        y_hi = jnp.dot(hid, wd_bf[:, half + c * chunk:half + (c + 1) * chunk], preferred_element_type=F32)
        words = _pack_bf16_pairs(y_lo, y_hi)
        for s in range(chunk // LANES):
            y_ref[pl.ds(c * (chunk // LANES) + s, tm, stride=REC_ROWS), :] = words[:, s * LANES:(s + 1) * LANES]

    @pl.when(i == last)
    def _():
        _wait_record_gather(tm, x_hbm, xbuf.at[1 - slot], sem.at[1 - slot])


def _moe_experts(x_rec, tile_expert, tok_slot, layer, w_gate, w_up, w_down):
    t_max, _, tm = tok_slot.shape
    _, e, d, f = w_gate.shape
    tok_spec = lambda shift: pl.BlockSpec(
        (1, 1, tm), functools.partial(lambda i, te, shift: (jnp.minimum(i + shift, t_max - 1), 0, 0), shift=shift),
        memory_space=pltpu.SMEM)
    grid_spec = pltpu.PrefetchScalarGridSpec(
        num_scalar_prefetch=1,
        grid=(t_max,),
        in_specs=[tok_spec(0), tok_spec(1),
                  pl.BlockSpec(memory_space=pl.ANY),
                  pl.BlockSpec((1, 1, d, f), lambda i, te: (layer, te[i], 0, 0)),
                  pl.BlockSpec((1, 1, d, f), lambda i, te: (layer, te[i], 0, 0)),
                  pl.BlockSpec((1, 1, f, d), lambda i, te: (layer, te[i], 0, 0))],
        out_specs=pl.BlockSpec((tm * REC_ROWS, LANES), lambda i, te: (i, 0)),
        scratch_shapes=[pltpu.VMEM((2, tm * GATHER_PITCH, LANES), jnp.int32), pltpu.SemaphoreType.DMA((2,)),
                        pltpu.VMEM((d, f), BF16), pltpu.VMEM((d, f), BF16), pltpu.VMEM((f, d), BF16)],
    )
    return pl.pallas_call(
        _moe_expert_kernel,
        grid_spec=grid_spec,
        out_shape=jax.ShapeDtypeStruct((t_max * tm * REC_ROWS, LANES), jnp.int32),
        compiler_params=_cparams(("arbitrary",), 56),
        name="moe_experts",
    )(tile_expert, tok_slot, tok_slot, x_rec, w_gate, w_up, w_down)


def _moe_combine_kernel(*refs, has_next, n_steps):
    if has_next:
        (slot_ref, slot_next_ref, y_hbm, wgt_ref, hs_ref, wsd_ref, xres, gate, gpost, gnext, sh, sc,
         xout, hout, ybuf, sem, yacc) = refs
    else:
        slot_ref, slot_next_ref, y_hbm, wgt_ref, hs_ref, wsd_ref, xres, gate, gpost, xout, ybuf, sem, yacc = refs
    i = pl.program_id(0)
    buf = lax.rem(i, 2)
    tmc = xout.shape[0]
    n_rec = TOP_K * tmc

    @pl.when(i == 0)
    def _():
        _start_record_gather(slot_ref, n_rec, y_hbm, ybuf.at[0], sem.at[0], GATHER_PITCH)

    _wait_record_gather(n_rec, y_hbm, ybuf.at[buf], sem.at[buf])
    d = xout.shape[1]
    half = d // 2
    wgt = wgt_ref[...]
    cur = ybuf.at[buf]
    for k0 in range(0, TOP_K, 2):
        wa = jnp.broadcast_to(wgt[:, k0:k0 + 1], (tmc, LANES))
        wb = jnp.broadcast_to(wgt[:, k0 + 1:k0 + 2], (tmc, LANES))
        for s in range(REC_ROWS):
            lo_a, hi_a = _unpack_bf16_pairs(cur[pl.ds(k0 * tmc * GATHER_PITCH + s, tmc, stride=GATHER_PITCH), :])
            lo_b, hi_b = _unpack_bf16_pairs(cur[pl.ds((k0 + 1) * tmc * GATHER_PITCH + s, tmc, stride=GATHER_PITCH), :])
            lo = wa * lo_a + wb * lo_b
            hi = wa * hi_a + wb * hi_b
            lo_cols = slice(s * LANES, (s + 1) * LANES)
            hi_cols = slice(half + s * LANES, half + (s + 1) * LANES)
            if k0 == 0:
                yacc[:, lo_cols] = lo
                yacc[:, hi_cols] = hi
            else:
                yacc[:, lo_cols] += lo
                yacc[:, hi_cols] += hi
        _issue_records(slot_next_ref, k0 * tmc, (k0 + 2) * tmc, y_hbm, ybuf.at[1 - buf], sem.at[1 - buf], GATHER_PITCH)
    hs = hs_ref[...]
    chunk = 4 * LANES
    for c in range(d // chunk):
        cols = slice(c * chunk, (c + 1) * chunk)
        yacc[:, cols] += jnp.dot(hs, wsd_ref[:, cols], preferred_element_type=F32)
    for r in range(tmc // 8):
        rows = slice(r * 8, (r + 1) * 8)
        y = yacc[rows, :]
        xn = xres[rows, :] + gate[0] * ((y * _rms_scale(y)) * gpost[...])
        xout[rows, :] = xn
        if has_next:
            yacc[rows, :] = ((xn * _rms_scale(xn)) * gnext[...]) * (1.0 + sc[0]) + sh[0]
    if has_next:
        hout[...] = yacc[...].astype(hout.dtype)

    @pl.when(i == n_steps - 1)
    def _():
        _wait_record_gather(n_rec, y_hbm, ybuf.at[1 - buf], sem.at[1 - buf])


def _moe_combine(y_rec, slot_tiles, wgt, hs, w_sd, xres, gate, gpost, nxt):
    m, d = xres.shape
    n_steps, _, n_rec = slot_tiles.shape
    tmc = n_rec // TOP_K
    f = hs.shape[1]
    has_next = nxt is not None
    slot_spec = lambda shift: pl.BlockSpec(
        (1, 1, n_rec), functools.partial(lambda i, shift: (jnp.minimum(i + shift, n_steps - 1), 0, 0), shift=shift),
        memory_space=pltpu.SMEM)
    vec_spec = pl.BlockSpec((1, d), lambda i: (0, 0))
    mod_spec = pl.BlockSpec((1, 1, d), lambda i: (i * tmc // SEQ, 0, 0))
    row_spec = pl.BlockSpec((tmc, d), lambda i: (i, 0))
    in_specs = [slot_spec(0), slot_spec(1), pl.BlockSpec(memory_space=pl.ANY),
                pl.BlockSpec((tmc, TOP_K), lambda i: (i, 0)), pl.BlockSpec((tmc, f), lambda i: (i, 0)),
                pl.BlockSpec((f, d), lambda i: (0, 0)), row_spec, mod_spec, vec_spec]
    args = [slot_tiles, slot_tiles, y_rec, wgt, hs, w_sd, xres, gate, gpost.reshape(1, d)]
    out_specs = [row_spec]
    out_shape = [jax.ShapeDtypeStruct((m, d), F32)]
    if has_next:
        gnext, sh, sc = nxt
        in_specs += [vec_spec, mod_spec, mod_spec]
        args += [gnext.reshape(1, d), sh, sc]
        out_specs.append(row_spec)
        out_shape.append(jax.ShapeDtypeStruct((m, d), BF16))
    outs = pl.pallas_call(
        functools.partial(_moe_combine_kernel, has_next=has_next, n_steps=n_steps),
        grid=(n_steps,),
        in_specs=in_specs,
        out_specs=out_specs,
        out_shape=out_shape,
        scratch_shapes=[pltpu.VMEM((2, n_rec * GATHER_PITCH, LANES), jnp.int32), pltpu.SemaphoreType.DMA((2,)),
                        pltpu.VMEM((tmc, d), F32)],
        compiler_params=_cparams(("arbitrary",), 48),
        name="moe_combine",
    )(*args)
    return (outs[0], outs[1]) if has_next else (outs[0], None)


def _mod_parts(mods_layer, n_rows):
    return [mods_layer[:n_rows, j * D_MODEL:(j + 1) * D_MODEL].reshape(n_rows, 1, D_MODEL) for j in range(6)]


def _moe(h2, h2_rec, x1, gate2, gpost, nxt, router_w, router_bias, layer, w_gate, w_up, w_down, sh_gate, sh_up, sh_down):
    n = h2.shape[0]
    tm, tmc = MOE_TILE, COMBINE_TOKENS
    t_max = n * TOP_K // tm + N_EXPERTS
    idx, wgt, pos, counts, hs = _router(h2, router_w, router_bias, sh_gate, sh_up)
    tiles = (counts[:, 0] + tm - 1) // tm
    tile_end = jnp.cumsum(tiles)
    n_tiles = tile_end[-1]
    first_slot = (tile_end - tiles) * tm
    slot = jnp.sum(jnp.where(idx[:, :, None] == jnp.arange(N_EXPERTS)[None, None, :], first_slot[None, None, :], 0),
                   axis=-1) + pos
    tok = jnp.broadcast_to(jnp.arange(n, dtype=jnp.int32)[None, :], slot.shape)
    tok_slot = jnp.zeros((t_max * tm,), jnp.int32).at[slot.reshape(-1)].set(tok.reshape(-1), unique_indices=True)
    tile_id = jnp.minimum(jnp.arange(t_max), n_tiles - 1)
    tile_expert = jnp.sum(tile_id[:, None] >= tile_end[None, :], axis=1).astype(jnp.int32)
    y_rec = _moe_experts(h2_rec, tile_expert, tok_slot.reshape(t_max, 1, tm), layer, w_gate, w_up, w_down)
    slot_tiles = slot.reshape(TOP_K, n // tmc, tmc).transpose(1, 0, 2).reshape(n // tmc, 1, TOP_K * tmc)
    return _moe_combine(y_rec, slot_tiles, wgt.T, hs, sh_down.astype(BF16), x1, gate2, gpost, nxt)


def kernel(x, c, ctx, c_ctx, ada_w, ada_b, norm_g, attn_w_in, attn_w_out, na_rpb, q_norm_g, k_norm_g, mix_w_in, mix_w_out, conv_w, sgu_w, sgu_b, sgu_norm_g, router_w, router_bias, moe_w_gate, moe_w_up, moe_w_down, shared_w_gate, shared_w_up, shared_w_down):
    nb, s, d = x.shape
    assert (s, d) == (SEQ, D_MODEL) and ctx.shape == (nb, CTX_LEN, d)
    x_lat = x.reshape(nb * s, d)
    x_ctx = ctx.reshape(nb * CTX_LEN, d)

    c_rows = jnp.concatenate([c, c_ctx[None, :], jnp.zeros((16 - nb - 1, d), F32)], axis=0)
    mods = _adaln(c_rows, ada_w, ada_b)
    lat_mods = [_mod_parts(mods[i], nb) for i in range(DEPTH)]
    ctx_mods = _mod_parts(mods[0, nb:nb + 1], 1)

    cos, sin = _rope_tables()
    bands, pats, na_idx = _na_static_tables()

    sh1, sc1 = lat_mods[0][0], lat_mods[0][1]
    h_lat = _norm_mod(x_lat, norm_g[0, 0], sh1, sc1, SEQ)

    for i in range(DEPTH):
        _, _, g1, sh2, sc2, g2 = lat_mods[i]
        if i % 2 == 0:
            e = i // 2
            w_in = attn_w_in[e]
            kv_lo, kv_hi = NA_DIM, 3 * NA_DIM
            gk_lo = 3 * NA_DIM + GQA_Q_DIM
            h_ctx = _norm_mod(x_ctx, norm_g[i, 0], ctx_mods[0], ctx_mods[1], nb * CTX_LEN)
            w_ctx = jnp.concatenate([w_in[:, kv_lo:kv_hi], w_in[:, gk_lo:]], axis=1).astype(BF16)
            p_lat = _matmul(h_lat, w_in.astype(BF16))
            p_ctx = _matmul(h_ctx, w_ctx)
            blk = 4 * HEAD_DIM
            qg = _qk_prep(p_lat, kv_hi // blk, GQA_Q_DIM // blk, q_norm_g[e], cos, sin, True)
            kg = _qk_prep(p_lat, gk_lo // blk, GQA_KV_DIM // blk, k_norm_g[e], cos, sin, True)
            kcg = _qk_prep(p_ctx, 2 * NA_DIM // blk, GQA_KV_DIM // blk, k_norm_g[e], cos, sin, False)
            bias = _na_bias_table(na_rpb[e], na_idx)
            o_na = _na_attention(p_lat, p_ctx, bias, bands, pats)
            o_gqa = _gqa_attention(qg, kg, p_lat, kcg, p_ctx,
                                   (gk_lo + GQA_KV_DIM) // HEAD_DIM, (2 * NA_DIM + GQA_KV_DIM) // HEAD_DIM)
            parts = [o_na, o_gqa]
            w_out = attn_w_out[e].astype(BF16)
        else:
            o = i // 2
            p_mix = _matmul(h_lat, mix_w_in[o].astype(BF16))
            parts = [_conv_sgu(p_mix, conv_w[o], sgu_w[o], sgu_b[o], sgu_norm_g[o])]
            wo = mix_w_out[o]
            w_out = jnp.concatenate(
                [wo[:CONV_DIM].reshape(SGU_GROUPS, SGU_CH, d), wo[CONV_DIM:].reshape(SGU_GROUPS, SGU_CH, d)],
                axis=1).reshape(CONV_DIM + SGU_DIM, d).astype(BF16)
        x1, h2, h2_rec = _matmul_post(parts, w_out, x_lat, g1, norm_g[i, 1], norm_g[i, 2], sh2, sc2)
        nxt = None
        if i + 1 < DEPTH:
            nxt = (norm_g[i + 1, 0], lat_mods[i + 1][0], lat_mods[i + 1][1])
        x_lat, h_lat = _moe(h2, h2_rec, x1, g2, norm_g[i, 3], nxt, router_w[i], router_bias[i],
                            i, moe_w_gate, moe_w_up, moe_w_down,
                            shared_w_gate[i], shared_w_up[i], shared_w_down[i])
    return x_lat.reshape(nb, s, d)
```

```python
import functools

import numpy as np
import jax
import jax.numpy as jnp
from jax import lax
from jax.experimental import pallas as pl
from jax.experimental.pallas import tpu as pltpu

F32 = jnp.float32
BF16 = jnp.bfloat16

D_MODEL = 4096
SEQ = 2048
DEPTH = 2
GRID_W = 64
GRID_ROWS = SEQ // GRID_W
CTX_LEN = 256
HEAD_DIM = 128
N_NA_HEADS = 16
N_GQA_HEADS = 16
N_GQA_KV = 4
GQA_REP = N_GQA_HEADS // N_GQA_KV
NA_KH = 8
NA_KW = 16
NA_BAND_ROWS = NA_KH + 1
NA_KEYS = NA_BAND_ROWS * GRID_W
NA_QBLK = 2 * GRID_W
NA_NBLK = GRID_ROWS // 2
ROPE_THETA = 10000.0
CONV_DIM = D_MODEL // 2
SGU_DIM = D_MODEL // 2
SGU_GROUPS = 8
SGU_CH = SGU_DIM // SGU_GROUPS
SGU_CHUNK = 128
N_EXPERTS = 64
N_EXPERT_GROUPS = 8
EXPERTS_PER_GROUP = N_EXPERTS // N_EXPERT_GROUPS
TOPK_GROUPS = 4
TOP_K = 8
D_EXPERT = D_MODEL // 16
ROUTED_SCALE = 2.5
EPS = 1e-6
NEG = -1e30
NA_DIM = N_NA_HEADS * HEAD_DIM
GQA_Q_DIM = N_GQA_HEADS * HEAD_DIM
GQA_KV_DIM = N_GQA_KV * HEAD_DIM
ATTN_SCALE = HEAD_DIM ** -0.5
LANES = 128
REC_ROWS = D_MODEL // 2 // LANES
MOE_TILE = 256
RECORD_PITCH = 20
COMBINE_TOKENS = 64
EXPERT_RING = 3

V7X_VMEM_BYTES = 64 * 1024 * 1024
MIB = 1024 * 1024


def _cparams(semantics, vmem_mib):
    assert vmem_mib * MIB < V7X_VMEM_BYTES
    return pltpu.CompilerParams(dimension_semantics=semantics, vmem_limit_bytes=vmem_mib * MIB)


def _rms_scale(x):
    return lax.rsqrt(jnp.mean(x * x, axis=-1, keepdims=True) + EPS)


def _silu(x):
    return x / (1.0 + jnp.exp(-x))


def _adaln_kernel(c_ref, w_ref, b_ref, o_ref):
    a = _silu(c_ref[...])
    o_ref[0] = jnp.dot(a.astype(BF16), w_ref[0].astype(BF16), preferred_element_type=F32) + b_ref[0]


def _adaln(c_rows, ada_w, ada_b):
    depth, d, n6 = ada_w.shape
    rows = c_rows.shape[0]
    tn = 512
    return pl.pallas_call(
        _adaln_kernel,
        grid=(depth, n6 // tn),
        in_specs=[pl.BlockSpec((rows, d), lambda l, j: (0, 0)),
                  pl.BlockSpec((1, d, tn), lambda l, j: (l, 0, j)),
                  pl.BlockSpec((1, 1, tn), lambda l, j: (l, 0, j))],
        out_specs=pl.BlockSpec((1, rows, tn), lambda l, j: (l, 0, j)),
        out_shape=jax.ShapeDtypeStruct((depth, rows, n6), F32),
        compiler_params=_cparams(("parallel", "parallel"), 40),
        name="adaln",
    )(c_rows, ada_w, ada_b.reshape(depth, 1, n6))


def _norm_mod_kernel(x_ref, g_ref, sh_ref, sc_ref, o_ref):
    x = x_ref[...]
    y = (x * _rms_scale(x)) * g_ref[...]
    o_ref[...] = (y * (1.0 + sc_ref[0]) + sh_ref[0]).astype(o_ref.dtype)


def _norm_mod(x, g, sh, sc, rows_per_mod):
    m, d = x.shape
    tm = 256
    mod_spec = pl.BlockSpec((1, 1, d), lambda i: (i * tm // rows_per_mod, 0, 0))
    return pl.pallas_call(
        _norm_mod_kernel,
        grid=(m // tm,),
        in_specs=[pl.BlockSpec((tm, d), lambda i: (i, 0)),
                  pl.BlockSpec((1, d), lambda i: (0, 0)),
                  mod_spec, mod_spec],
        out_specs=pl.BlockSpec((tm, d), lambda i: (i, 0)),
        out_shape=jax.ShapeDtypeStruct((m, d), BF16),
        compiler_params=_cparams(("parallel",), 40),
        name="norm_mod",
    )(x, g.reshape(1, d), sh, sc)


def _mm_kernel(x_ref, w_ref, o_ref):
    o_ref[...] = jnp.dot(x_ref[...], w_ref[...], preferred_element_type=F32).astype(o_ref.dtype)


def _matmul(x, w, tm=1024, tn=512):
    m, k = x.shape
    n = w.shape[1]
    tm = min(tm, m)
    return pl.pallas_call(
        _mm_kernel,
        grid=(m // tm, n // tn),
        in_specs=[pl.BlockSpec((tm, k), lambda i, j: (i, 0)),
                  pl.BlockSpec((k, tn), lambda i, j: (0, j))],
        out_specs=pl.BlockSpec((tm, tn), lambda i, j: (i, j)),
        out_shape=jax.ShapeDtypeStruct((m, n), BF16),
        compiler_params=_cparams(("parallel", "arbitrary"), 48),
        name="matmul",
    )(x, w)


def _pack_bf16_pairs(lo, hi):
    return pltpu.pack_elementwise([lo, hi], packed_dtype=BF16)


def _unpack_bf16_pairs(words):
    lo = pltpu.unpack_elementwise(words, index=0, packed_dtype=BF16, unpacked_dtype=F32)
    hi = pltpu.unpack_elementwise(words, index=1, packed_dtype=BF16, unpacked_dtype=F32)
    return lo, hi


def _store_records(ref, first_row, words, pitch):
    n = words.shape[0]
    for s in range(REC_ROWS):
        ref[pl.ds(first_row + s, n, stride=pitch), :] = words[:, s * LANES:(s + 1) * LANES]
    for s in range(REC_ROWS, pitch):
        ref[pl.ds(first_row + s, n, stride=pitch), :] = jnp.zeros((n, LANES), words.dtype)


def _load_records(ref, first_row, n, pitch):
    return jnp.concatenate([ref[pl.ds(first_row + s, n, stride=pitch), :] for s in range(REC_ROWS)], axis=1)


def _mm_post_kernel(*refs, n_parts, nk, n_chunks, row_chunk):
    lhs = refs[:n_parts]
    ws = refs[n_parts:2 * n_parts]
    xres, gate, gpost, gnext, sh, sc, xout, hout, hrec, vecs = refs[2 * n_parts:]
    k = pl.program_id(1)
    tm, d = xout.shape
    tn = d // n_chunks

    def part(c):
        acc = None
        for l, w in zip(lhs, ws):
            t = jnp.dot(l[...], w[:, c * tn:(c + 1) * tn], preferred_element_type=F32)
            acc = t if acc is None else acc + t
        return acc

    @pl.when(k == 0)
    def _():
        for c in range(n_chunks):
            xout[:, c * tn:(c + 1) * tn] = part(c)

    @pl.when(k > 0)
    def _():
        for c in range(n_chunks):
            xout[:, c * tn:(c + 1) * tn] += part(c)

    @pl.when(k == nk - 1)
    def _():
        vecs[0:1, :] = gate[0] * gpost[...]
        vecs[1:2, :] = gnext[...] * (1.0 + sc[0])

        def body(r, carry):
            rows = pl.ds(pl.multiple_of(r * row_chunk, row_chunk), row_chunk)
            y = xout[rows, :]
            xn = xres[rows, :] + (y * _rms_scale(y)) * vecs[0:1, :]
            xout[rows, :] = xn
            h = (xn * _rms_scale(xn)) * vecs[1:2, :] + sh[0]
            hout[rows, :] = h.astype(hout.dtype)
            half = d // 2
            first = pl.multiple_of(r * (row_chunk * RECORD_PITCH), row_chunk * RECORD_PITCH)
            _store_records(hrec, first, _pack_bf16_pairs(h[:, :half], h[:, half:]), RECORD_PITCH)
            return carry
        lax.fori_loop(0, tm // row_chunk, body, 0)


def _matmul_post(lhs_parts, w, xres, gate, gpost, gnext, sh, sc, tm=512):
    m, d = xres.shape
    n_parts = len(lhs_parts)
    kp = lhs_parts[0].shape[1]
    assert all(l.shape == (m, kp) for l in lhs_parts) and w.shape == (n_parts * kp, d)
    tk = 512 // n_parts
    nk = kp // tk
    vec_spec = pl.BlockSpec((1, d), lambda i, k: (0, 0))
    mod_spec = pl.BlockSpec((1, 1, d), lambda i, k: (i * tm // SEQ, 0, 0))
    row_spec = pl.BlockSpec((tm, d), lambda i, k: (i, 0))
    in_specs = [pl.BlockSpec((tm, tk), lambda i, k: (i, k)) for _ in range(n_parts)]
    in_specs += [pl.BlockSpec((tk, d), functools.partial(lambda i, k, off: (k + off, 0), off=p * nk))
                 for p in range(n_parts)]
    in_specs += [pl.BlockSpec((tm, d), lambda i, k: (i, 0), pipeline_mode=pl.Buffered(1)),
                 mod_spec, vec_spec, vec_spec, mod_spec, mod_spec]
    args = list(lhs_parts) + [w] * n_parts + [xres, gate, gpost.reshape(1, d), gnext.reshape(1, d), sh, sc]
    return pl.pallas_call(
        functools.partial(_mm_post_kernel, n_parts=n_parts, nk=nk, n_chunks=4, row_chunk=32),
        grid=(m // tm, nk),
        in_specs=in_specs,
        out_specs=[row_spec, row_spec, pl.BlockSpec((tm * RECORD_PITCH, LANES), lambda i, k: (i, 0))],
        out_shape=[jax.ShapeDtypeStruct((m, d), F32), jax.ShapeDtypeStruct((m, d), BF16),
                   jax.ShapeDtypeStruct((m * RECORD_PITCH, LANES), jnp.int32)],
        scratch_shapes=[pltpu.VMEM((8, d), F32)],
        compiler_params=_cparams(("parallel", "arbitrary"), 56),
        name="matmul_post",
    )(*args)


def _qk_prep_kernel(x_ref, g_ref, cos_ref, sin_ref, o_ref, *, rope):
    n_heads = x_ref.shape[1] // HEAD_DIM
    for hh in range(n_heads):
        cols = slice(hh * HEAD_DIM, (hh + 1) * HEAD_DIM)
        x = x_ref[:, cols].astype(F32)
        y = (x * _rms_scale(x)) * g_ref[...]
        if rope:
            lane = lax.broadcasted_iota(jnp.int32, y.shape, 1)
            quarter = HEAD_DIM // 4
            partner = jnp.where(jnp.bitwise_and(lane, 2 * quarter - 1) < quarter,
                                pltpu.roll(y, HEAD_DIM - quarter, axis=1), pltpu.roll(y, quarter, axis=1))
            y = y * cos_ref[...] + partner * sin_ref[...]
        o_ref[:, cols] = y.astype(o_ref.dtype)


def _qk_prep(p, col_block0, n_col_blocks, g, cos, sin, rope):
    m = p.shape[0]
    tm, tc = 512, 4 * HEAD_DIM
    pos_blocks = SEQ // tm
    return pl.pallas_call(
        functools.partial(_qk_prep_kernel, rope=rope),
        grid=(m // tm, n_col_blocks),
        in_specs=[pl.BlockSpec((tm, tc), lambda i, j: (i, col_block0 + j)),
                  pl.BlockSpec((1, HEAD_DIM), lambda i, j: (0, 0)),
                  pl.BlockSpec((tm, HEAD_DIM), lambda i, j: (i % pos_blocks, 0)),
                  pl.BlockSpec((tm, HEAD_DIM), lambda i, j: (i % pos_blocks, 0))],
        out_specs=pl.BlockSpec((tm, tc), lambda i, j: (i, j)),
        out_shape=jax.ShapeDtypeStruct((m, tc * n_col_blocks), BF16),
        compiler_params=_cparams(("parallel", "parallel"), 32),
        name="qk_prep",
    )(p, g.reshape(1, HEAD_DIM), cos, sin)


def _rope_tables():
    half = HEAD_DIM // 4
    t = jnp.arange(SEQ)
    freqs = ROPE_THETA ** (-jnp.arange(half, dtype=F32) / half)
    ang_r = (t // GRID_W).astype(F32)[:, None] * freqs[None, :]
    ang_c = (t % GRID_W).astype(F32)[:, None] * freqs[None, :]
    cos = jnp.concatenate([jnp.cos(ang_r)] * 2 + [jnp.cos(ang_c)] * 2, axis=-1)
    sin = jnp.concatenate([-jnp.sin(ang_r), jnp.sin(ang_r), -jnp.sin(ang_c), jnp.sin(ang_c)], axis=-1)
    return cos, sin


def _na_static_tables():
    rows, kh, kw = GRID_ROWS, min(NA_KH, GRID_ROWS), NA_KW
    nb_rows = min(kh + 1, rows)
    assert nb_rows == NA_BAND_ROWS
    col = np.arange(GRID_W)
    col_start = np.clip(col - kw // 2, 0, GRID_W - kw)
    patterns, pat_of_block, bands = [], [], []
    for b in range(NA_NBLK):
        q_rows = 2 * b + np.arange(2)
        row_start = np.clip(q_rows - kh // 2, 0, rows - kh)
        band = int(np.clip(row_start[0], 0, rows - nb_rows))
        key = (tuple(q_rows - band), tuple(row_start - band))
        if key not in patterns:
            patterns.append(key)
        pat_of_block.append(patterns.index(key))
        bands.append(band)
    idx = np.full((len(patterns), NA_QBLK, NA_KEYS), -1, np.int32)
    for p, (q_rel, rs_rel) in enumerate(patterns):
        qr = np.repeat(np.array(q_rel), GRID_W)
        qc = np.tile(col, 2)
        rs = np.repeat(np.array(rs_rel), GRID_W)
        cs = np.tile(col_start, 2)
        kr = np.repeat(np.arange(nb_rows), GRID_W)
        kc = np.tile(col, nb_rows)
        in_win = ((kr[None, :] >= rs[:, None]) & (kr[None, :] < rs[:, None] + kh)
                  & (kc[None, :] >= cs[:, None]) & (kc[None, :] < cs[:, None] + kw))
        dr = np.clip(kr[None, :] - qr[:, None] + NA_KH - 1, 0, 2 * NA_KH - 2)
        dc = np.clip(kc[None, :] - qc[:, None] + NA_KW - 1, 0, 2 * NA_KW - 2)
        idx[p] = np.where(in_win, dr * (2 * NA_KW - 1) + dc, -1)
    return np.array(bands, np.int32), np.array(pat_of_block, np.int32), idx


def _na_bias_kernel(idx_ref, rpb_ref, o_ref):
    idx = idx_ref[0]
    n_tab = rpb_ref.shape[1]
    onehot = (lax.broadcasted_iota(jnp.int32, (n_tab, idx.shape[1]), 0) == idx).astype(BF16)
    r = rpb_ref[...]
    hi = r.astype(BF16)
    r1 = r - hi.astype(F32)
    mid = r1.astype(BF16)
    lo = (r1 - mid.astype(F32)).astype(BF16)
    val = (jnp.dot(hi, onehot, preferred_element_type=F32) + jnp.dot(mid, onehot, preferred_element_type=F32)
           + jnp.dot(lo, onehot, preferred_element_type=F32))
    o_ref[0] = jnp.where(idx < 0, NEG, val)


def _na_bias_table(rpb, idx):
    n_pat = idx.shape[0]
    h = rpb.shape[0]
    n_tab = 512
    flat = NA_QBLK * NA_KEYS
    tile = 8 * NA_KEYS
    rpb_flat = jnp.pad(rpb.reshape(h, -1), ((0, 0), (0, n_tab - rpb.shape[1] * rpb.shape[2])))
    out = pl.pallas_call(
        _na_bias_kernel,
        grid=(n_pat, flat // tile),
        in_specs=[pl.BlockSpec((1, 1, tile), lambda p, j: (p, 0, j)),
                  pl.BlockSpec((h, n_tab), lambda p, j: (0, 0))],
        out_specs=pl.BlockSpec((1, h, tile), lambda p, j: (p, 0, j)),
        out_shape=jax.ShapeDtypeStruct((n_pat, h, flat), F32),
        compiler_params=_cparams(("parallel", "parallel"), 32),
        name="na_bias",
    )(jnp.asarray(idx.reshape(n_pat, 1, flat)), rpb_flat)
    return out.reshape(n_pat, h, NA_QBLK, NA_KEYS)


_NT = (((1,), (1,)), ((), ()))


def _na_attn_kernel(q_ref, k_ref, v_ref, kc_ref, vc_ref, bias_ref, o_ref, *, bands, pats):
    kc = kc_ref[...]
    vc = vc_ref[...]

    for n in range(NA_NBLK):
        q_rows = slice(n * NA_QBLK, (n + 1) * NA_QBLK)
        key_rows = slice(bands[n] * GRID_W, bands[n] * GRID_W + NA_KEYS)
        q = q_ref[q_rows, :]
        bias = bias_ref[pats[n], 0]
        s = lax.dot_general(q, k_ref[key_rows, :], _NT, preferred_element_type=F32) * ATTN_SCALE + bias
        s = jnp.where(bias <= 0.5 * NEG, NEG, s)
        s_ctx = lax.dot_general(q, kc, _NT, preferred_element_type=F32) * ATTN_SCALE
        m = jnp.maximum(jnp.max(s, axis=-1, keepdims=True), jnp.max(s_ctx, axis=-1, keepdims=True))
        p = jnp.exp(s - m)
        p_ctx = jnp.exp(s_ctx - m)
        denom = jnp.sum(p, axis=-1, keepdims=True) + jnp.sum(p_ctx, axis=-1, keepdims=True)
        o = (jnp.dot(p.astype(BF16), v_ref[key_rows, :], preferred_element_type=F32)
             + jnp.dot(p_ctx.astype(BF16), vc, preferred_element_type=F32))
        o_ref[q_rows, :] = (o / denom).astype(o_ref.dtype)


def _na_attention(p_lat, p_ctx, bias, bands, pats):
    n_tok = p_lat.shape[0]
    nb = n_tok // SEQ
    n_pat = bias.shape[0]
    hd = HEAD_DIM
    return pl.pallas_call(
        functools.partial(_na_attn_kernel, bands=tuple(int(b) for b in bands), pats=tuple(int(p) for p in pats)),
        grid=(N_NA_HEADS, nb),
        in_specs=[pl.BlockSpec((SEQ, hd), lambda h, b: (b, h)),
                  pl.BlockSpec((SEQ, hd), lambda h, b: (b, N_NA_HEADS + h)),
                  pl.BlockSpec((SEQ, hd), lambda h, b: (b, 2 * N_NA_HEADS + h)),
                  pl.BlockSpec((CTX_LEN, hd), lambda h, b: (b, h)),
                  pl.BlockSpec((CTX_LEN, hd), lambda h, b: (b, N_NA_HEADS + h)),
                  pl.BlockSpec((n_pat, 1, NA_QBLK, NA_KEYS), lambda h, b: (0, h, 0, 0))],
        out_specs=pl.BlockSpec((SEQ, hd), lambda h, b: (b, h)),
        out_shape=jax.ShapeDtypeStruct((n_tok, NA_DIM), BF16),
        compiler_params=_cparams(("parallel", "parallel"), 32),
        name="na_attention",
    )(p_lat, p_lat, p_lat, p_ctx, p_ctx, bias)


def _gqa_attn_kernel(q_ref, k_ref, v_ref, kc_ref, vc_ref, o_ref):
    k = k_ref[...]
    v = v_ref[...]
    kc = kc_ref[...]
    vc = vc_ref[...]
    for r in range(GQA_REP):
        cols = slice(r * HEAD_DIM, (r + 1) * HEAD_DIM)
        q = q_ref[:, cols]
        s = lax.dot_general(q, k, _NT, preferred_element_type=F32) * ATTN_SCALE
        s_ctx = lax.dot_general(q, kc, _NT, preferred_element_type=F32) * ATTN_SCALE
        m = jnp.maximum(jnp.max(s, axis=-1, keepdims=True), jnp.max(s_ctx, axis=-1, keepdims=True))
        p = jnp.exp(s - m)
        p_ctx = jnp.exp(s_ctx - m)
        denom = jnp.sum(p, axis=-1, keepdims=True) + jnp.sum(p_ctx, axis=-1, keepdims=True)
        o = (jnp.dot(p.astype(BF16), v, preferred_element_type=F32)
             + jnp.dot(p_ctx.astype(BF16), vc, preferred_element_type=F32))
        o_ref[:, cols] = (o / denom).astype(o_ref.dtype)


def _gqa_attention(qg, kg, p_lat, kcg, p_ctx, v_col_block, vc_col_block, tq=256):
    n_tok = qg.shape[0]
    nb = n_tok // SEQ
    hd = HEAD_DIM
    qt = SEQ // tq
    return pl.pallas_call(
        _gqa_attn_kernel,
        grid=(nb, N_GQA_KV, qt),
        in_specs=[pl.BlockSpec((tq, GQA_REP * hd), lambda b, g, t: (b * qt + t, g)),
                  pl.BlockSpec((SEQ, hd), lambda b, g, t: (b, g)),
                  pl.BlockSpec((SEQ, hd), lambda b, g, t: (b, v_col_block + g)),
                  pl.BlockSpec((CTX_LEN, hd), lambda b, g, t: (b, g)),
                  pl.BlockSpec((CTX_LEN, hd), lambda b, g, t: (b, vc_col_block + g))],
        out_specs=pl.BlockSpec((tq, GQA_REP * hd), lambda b, g, t: (b * qt + t, g)),
        out_shape=jax.ShapeDtypeStruct((n_tok, GQA_Q_DIM), BF16),
        compiler_params=_cparams(("parallel", "parallel", "parallel"), 48),
        name="gqa_attention",
    )(qg, kg, p_lat, kcg, p_ctx)


def _mixer_kernel(gb_ref, gc_ref, xc_ref, u_ref, v_ref, cw_ref, sw_ref, sb_ref, sg_ref, o_ref):
    s, c = gb_ref.shape
    z = gc_ref[...].astype(F32) * xc_ref[...].astype(F32)
    row = lax.broadcasted_iota(jnp.int32, (s, c), 0)
    z_prev = jnp.where(row == 0, 0.0, pltpu.roll(z, 1, axis=0))
    z_next = jnp.where(row == s - 1, 0.0, pltpu.roll(z, s - 1, axis=0))
    cw = cw_ref[...]
    conv = z_prev * cw[0:1, :] + z * cw[1:2, :] + z_next * cw[2:3, :]
    o_ref[:, :c] = (gb_ref[...].astype(F32) * conv).astype(o_ref.dtype)
    w = sw_ref[0].astype(BF16)
    for n in range(s // SGU_CHUNK):
        rows = slice(n * SGU_CHUNK, (n + 1) * SGU_CHUNK)
        vv = v_ref[rows, :].astype(F32)
        vg = (vv * _rms_scale(vv)) * sg_ref[0]
        mix = jnp.dot(w, vg.astype(BF16), preferred_element_type=F32) + sb_ref[0]
        o_ref[rows, c:] = (u_ref[rows, :].astype(F32) * mix).astype(o_ref.dtype)


def _conv_sgu(p, conv_w, sgu_w, sgu_b, sgu_g):
    n_tok = p.shape[0]
    nb = n_tok // SEQ
    c = SGU_CH
    blocks = CONV_DIM // c
    col = lambda off: pl.BlockSpec((SEQ, c), functools.partial(lambda b, g, off: (b, off + g), off=off))
    return pl.pallas_call(
        _mixer_kernel,
        grid=(nb, SGU_GROUPS),
        in_specs=[col(0), col(blocks), col(2 * blocks), col(3 * blocks), col(4 * blocks),
                  pl.BlockSpec((3, c), lambda b, g: (0, g)),
                  pl.BlockSpec((1, SGU_CHUNK, SGU_CHUNK), lambda b, g: (g, 0, 0)),
                  pl.BlockSpec((1, SGU_CHUNK, 1), lambda b, g: (g, 0, 0)),
                  pl.BlockSpec((1, 1, c), lambda b, g: (g, 0, 0))],
        out_specs=pl.BlockSpec((SEQ, 2 * c), lambda b, g: (b, g)),
        out_shape=jax.ShapeDtypeStruct((n_tok, CONV_DIM + SGU_DIM), BF16),
        compiler_params=_cparams(("parallel", "parallel"), 48),
        name="conv_sgu",
    )(p, p, p, p, p, conv_w, sgu_w, sgu_b.reshape(SGU_GROUPS, SGU_CHUNK, 1), sgu_g.reshape(SGU_GROUPS, 1, c))


def _router_kernel(x_ref, whi_ref, wlo_ref, bias_ref, tri_ref, wsh_ref,
                   idx_ref, wgt_ref, pos_ref, cnt_ref, hs_ref, carry_ref):
    @pl.when(pl.program_id(0) == 0)
    def _():
        carry_ref[...] = jnp.zeros_like(carry_ref)

    x = x_ref[...]
    hsu = jnp.dot(x, wsh_ref[...], preferred_element_type=F32)
    f = hsu.shape[1] // 2
    hs_ref[...] = (_silu(hsu[:, :f]) * hsu[:, f:]).astype(hs_ref.dtype)

    logits = (lax.dot_general(whi_ref[...], x, _NT, preferred_element_type=F32)
              + lax.dot_general(wlo_ref[...], x, _NT, preferred_element_type=F32))
    scores = 1.0 / (1.0 + jnp.exp(-logits))
    sel = scores + bias_ref[...]
    n_e, tm = sel.shape
    epg = EXPERTS_PER_GROUP
    groups = [sel[g * epg:(g + 1) * epg, :] for g in range(N_EXPERT_GROUPS)]
    gscore = []
    for sg in groups:
        top1 = jnp.max(sg, axis=0, keepdims=True)
        is_top = sg == top1
        n_top = jnp.sum(is_top.astype(F32), axis=0, keepdims=True)
        second = jnp.max(jnp.where(is_top, -jnp.inf, sg), axis=0, keepdims=True)
        gscore.append(top1 + jnp.where(n_top >= 2.0, top1, second))
    masked = []
    for g in range(N_EXPERT_GROUPS):
        rank = jnp.zeros((1, tm), F32)
        for o in range(N_EXPERT_GROUPS):
            if o == g:
                continue
            beats = (gscore[o] >= gscore[g]) if o < g else (gscore[o] > gscore[g])
            rank = rank + beats.astype(F32)
        masked.append(jnp.where(rank < float(TOPK_GROUPS), groups[g], -jnp.inf))
    v = jnp.concatenate(masked, axis=0)
    row = lax.broadcasted_iota(jnp.int32, (n_e, tm), 0)
    chosen = jnp.zeros((n_e, tm), F32)
    picks, top_s = [], []
    for _ in range(TOP_K):
        best = jnp.max(v, axis=0, keepdims=True)
        first = jnp.min(jnp.where(v == best, row, n_e), axis=0, keepdims=True)
        pick = row == first
        picks.append(first)
        top_s.append(jnp.sum(jnp.where(pick, scores, 0.0), axis=0, keepdims=True))
        chosen = jnp.where(pick, 1.0, chosen)
        v = jnp.where(pick, -jnp.inf, v)
    total = top_s[0]
    for s in top_s[1:]:
        total = total + s
    running = jnp.dot(chosen.astype(BF16), tri_ref[...], preferred_element_type=F32) + carry_ref[...]
    pos = [jnp.sum(jnp.where(row == first, running - 1.0, 0.0), axis=0, keepdims=True) for first in picks]
    idx_ref[...] = jnp.concatenate(picks, axis=0)
    wgt_ref[...] = jnp.concatenate([s / total * ROUTED_SCALE for s in top_s], axis=0)
    pos_ref[...] = jnp.concatenate(pos, axis=0).astype(jnp.int32)
    carry_ref[...] += jnp.sum(chosen, axis=1, keepdims=True)
    cnt_ref[...] = carry_ref[...].astype(jnp.int32)


def _router(h, router_w, router_bias, sh_gate, sh_up, tm=1024):
    n, d = h.shape
    f = sh_gate.shape[1]
    wt = router_w.T
    w_hi = wt.astype(BF16)
    w_lo = (wt - w_hi.astype(F32)).astype(BF16)
    tri = (jnp.arange(tm)[:, None] <= jnp.arange(tm)[None, :]).astype(BF16)
    w_sh = jnp.concatenate([sh_gate, sh_up], axis=1).astype(BF16)
    fixed = lambda shape: pl.BlockSpec(shape, lambda i: (0, 0))
    per_tok = pl.BlockSpec((TOP_K, tm), lambda i: (0, i))
    return pl.pallas_call(
        _router_kernel,
        grid=(n // tm,),
        in_specs=[pl.BlockSpec((tm, d), lambda i: (i, 0)), fixed((N_EXPERTS, d)), fixed((N_EXPERTS, d)),
                  fixed((N_EXPERTS, 1)), fixed((tm, tm)), fixed((d, 2 * f))],
        out_specs=[per_tok, per_tok, per_tok, fixed((N_EXPERTS, 1)), pl.BlockSpec((tm, f), lambda i: (i, 0))],
        out_shape=[jax.ShapeDtypeStruct((TOP_K, n), jnp.int32), jax.ShapeDtypeStruct((TOP_K, n), F32),
                   jax.ShapeDtypeStruct((TOP_K, n), jnp.int32), jax.ShapeDtypeStruct((N_EXPERTS, 1), jnp.int32),
                   jax.ShapeDtypeStruct((n, f), BF16)],
        scratch_shapes=[pltpu.VMEM((N_EXPERTS, 1), F32)],
        compiler_params=_cparams(("arbitrary",), 48),
        name="router",
    )(h, w_hi, w_lo, router_bias.reshape(N_EXPERTS, 1), tri, w_sh)


def _start_record_gather(idx_ref, n, src_hbm, dst, sem, pitch, unroll=8):
    def body(j, carry):
        for u in range(unroll):
            r = j * unroll + u
            src_row = pl.multiple_of(idx_ref[0, 0, r] * pitch, 4)
            dst_row = pl.multiple_of(r * pitch, 4)
            pltpu.make_async_copy(src_hbm.at[pl.ds(src_row, REC_ROWS), :], dst.at[pl.ds(dst_row, REC_ROWS), :], sem).start()
        return carry
    lax.fori_loop(0, n // unroll, body, 0)


def _wait_record_gather(n, src_hbm, dst, sem):
    pltpu.make_async_copy(src_hbm.at[pl.ds(0, n * REC_ROWS), :], dst.at[pl.ds(0, n * REC_ROWS), :], sem).wait()


def _issue_records(idx_ref, r0, r1, src_hbm, dst, sem, pitch):
    for r in range(r0, r1):
        src_row = pl.multiple_of(idx_ref[0, 0, r] * pitch, 4)
        pltpu.make_async_copy(src_hbm.at[pl.ds(src_row, REC_ROWS), :], dst.at[pl.ds(r * pitch, REC_ROWS), :], sem).start()


def _moe_expert_kernel(te_ref, tok_ref, tok_next_ref, tok_ahead_ref, x_hbm, wg_ref, wu_ref, wd_ref, y_ref,
                       xbuf, sem, wg_bf, wu_bf, wd_bf):
    i = pl.program_id(0)
    last = pl.num_programs(0) - 1
    slot = lax.rem(i, EXPERT_RING)
    slot_next = lax.rem(i + 1, EXPERT_RING)
    slot_ahead = lax.rem(i + 2, EXPERT_RING)
    tm = tok_ref.shape[2]
    half = D_MODEL // 2
    chunk = 4 * LANES
    n_chunks = half // chunk

    @pl.when(i == 0)
    def _():
        _start_record_gather(tok_ref, tm, x_hbm, xbuf.at[0], sem.at[0], RECORD_PITCH)
        _start_record_gather(tok_next_ref, tm, x_hbm, xbuf.at[1], sem.at[1], RECORD_PITCH)

    @pl.when(jnp.logical_or(i == 0, te_ref[i] != te_ref[jnp.maximum(i - 1, 0)]))
    def _():
        wg_bf[...] = wg_ref[0, 0].astype(BF16)
        wu_bf[...] = wu_ref[0, 0].astype(BF16)
        wd_bf[...] = wd_ref[0, 0].astype(BF16)

    _wait_record_gather(tm, x_hbm, xbuf.at[slot], sem.at[slot])
    lo, hi = _unpack_bf16_pairs(_load_records(xbuf.at[slot], 0, tm, RECORD_PITCH))
    lo = lo.astype(BF16)
    hi = hi.astype(BF16)
    n_groups = n_chunks + 2
    bounds = [tm * g // n_groups for g in range(n_groups + 1)]
    prefetch = lambda g: _issue_records(tok_ahead_ref, bounds[g], bounds[g + 1], x_hbm, xbuf.at[slot_ahead],
                                        sem.at[slot_ahead], RECORD_PITCH)
    prefetch(0)
    pre_g = (jnp.dot(lo, wg_bf[:half, :], preferred_element_type=F32)
             + jnp.dot(hi, wg_bf[half:, :], preferred_element_type=F32))
    prefetch(1)
    pre_u = (jnp.dot(lo, wu_bf[:half, :], preferred_element_type=F32)
             + jnp.dot(hi, wu_bf[half:, :], preferred_element_type=F32))
    hid = (_silu(pre_g) * pre_u).astype(BF16)
    for c in range(n_chunks):
        prefetch(2 + c)
        y_lo = jnp.dot(hid, wd_bf[:, c * chunk:(c + 1) * chunk], preferred_element_type=F32)
        y_hi = jnp.dot(hid, wd_bf[:, half + c * chunk:half + (c + 1) * chunk], preferred_element_type=F32)
        words = _pack_bf16_pairs(y_lo, y_hi)
        for s in range(chunk // LANES):
            y_ref[pl.ds(c * (chunk // LANES) + s, tm, stride=RECORD_PITCH), :] = words[:, s * LANES:(s + 1) * LANES]
    for s in range(REC_ROWS, RECORD_PITCH):
        y_ref[pl.ds(s, tm, stride=RECORD_PITCH), :] = jnp.zeros((tm, LANES), y_ref.dtype)

    @pl.when(i == last)
    def _():
        _wait_record_gather(tm, x_hbm, xbuf.at[slot_next], sem.at[slot_next])
        _wait_record_gather(tm, x_hbm, xbuf.at[slot_ahead], sem.at[slot_ahead])


def _moe_experts(x_rec, tile_expert, tok_slot, layer, w_gate, w_up, w_down):
    t_max, _, tm = tok_slot.shape
    _, e, d, f = w_gate.shape
    tok_spec = lambda shift: pl.BlockSpec(
        (1, 1, tm), functools.partial(lambda i, te, shift: (jnp.minimum(i + shift, t_max - 1), 0, 0), shift=shift),
        memory_space=pltpu.SMEM)
    grid_spec = pltpu.PrefetchScalarGridSpec(
        num_scalar_prefetch=1,
        grid=(t_max,),
        in_specs=[tok_spec(0), tok_spec(1), tok_spec(2),
                  pl.BlockSpec(memory_space=pl.ANY),
                  pl.BlockSpec((1, 1, d, f), lambda i, te: (layer, te[i], 0, 0)),
                  pl.BlockSpec((1, 1, d, f), lambda i, te: (layer, te[i], 0, 0)),
                  pl.BlockSpec((1, 1, f, d), lambda i, te: (layer, te[i], 0, 0))],
        out_specs=pl.BlockSpec((tm * RECORD_PITCH, LANES), lambda i, te: (i, 0)),
        scratch_shapes=[pltpu.VMEM((EXPERT_RING, tm * RECORD_PITCH, LANES), jnp.int32),
                        pltpu.SemaphoreType.DMA((EXPERT_RING,)),
                        pltpu.VMEM((d, f), BF16), pltpu.VMEM((d, f), BF16), pltpu.VMEM((f, d), BF16)],
    )
    return pl.pallas_call(
        _moe_expert_kernel,
        grid_spec=grid_spec,
        out_shape=jax.ShapeDtypeStruct((t_max * tm * RECORD_PITCH, LANES), jnp.int32),
        compiler_params=_cparams(("arbitrary",), 56),
        name="moe_experts",
    )(tile_expert, tok_slot, tok_slot, tok_slot, x_rec, w_gate, w_up, w_down)


def _moe_combine_kernel(*refs, has_next, n_steps):
    if has_next:
        (slot_ref, slot_next_ref, y_hbm, wgt_ref, hs_ref, wsd_ref, xres, gate, gpost, gnext, sh, sc,
         xout, hout, ybuf_even, ybuf_odd, sem, yacc, wbuf) = refs
    else:
        (slot_ref, slot_next_ref, y_hbm, wgt_ref, hs_ref, wsd_ref, xres, gate, gpost,
         xout, ybuf_even, ybuf_odd, sem, yacc, wbuf) = refs
    i = pl.program_id(0)
    odd = lax.rem(i, 2) == 1
    tmc = xout.shape[0]
    n_rec = TOP_K * tmc
    n_col = xout.shape[1] // LANES

    @pl.when(i == 0)
    def _():
        _start_record_gather(slot_ref, n_rec, y_hbm, ybuf_even, sem.at[0], RECORD_PITCH)

    hs = hs_ref[...]
    for c in range(n_col):
        yacc[c] = jnp.dot(hs, wsd_ref[:, c * LANES:(c + 1) * LANES], preferred_element_type=F32)
    wgt = wgt_ref[...]
    for k in range(TOP_K):
        wbuf[k] = jnp.broadcast_to(wgt[:, k:k + 1], (tmc, LANES))
    n_iter = TOP_K // 2
    per_s = n_rec // (n_iter * REC_ROWS)

    def accumulate(cur, cur_sem, nxt, nxt_sem):
        _wait_record_gather(n_rec, y_hbm, cur, cur_sem)

        def body(kp, carry):
            ka = 2 * kp
            wa = wbuf[ka]
            wb = wbuf[ka + 1]
            first = ka * (tmc * RECORD_PITCH)
            for s in range(REC_ROWS):
                lo_a, hi_a = _unpack_bf16_pairs(cur[pl.ds(first + s, tmc, stride=RECORD_PITCH), :])
                lo_b, hi_b = _unpack_bf16_pairs(
                    cur[pl.ds(first + tmc * RECORD_PITCH + s, tmc, stride=RECORD_PITCH), :])
                yacc[s] += wa * lo_a + wb * lo_b
                yacc[REC_ROWS + s] += wa * hi_a + wb * hi_b
                for u in range(per_s):
                    r = (kp * REC_ROWS + s) * per_s + u
                    src_row = pl.multiple_of(slot_next_ref[0, 0, r] * RECORD_PITCH, 4)
                    dst_row = pl.multiple_of(r * RECORD_PITCH, 4)
                    pltpu.make_async_copy(y_hbm.at[pl.ds(src_row, REC_ROWS), :], nxt.at[pl.ds(dst_row, REC_ROWS), :],
                                          nxt_sem).start()
            return carry
        lax.fori_loop(0, n_iter, body, 0)

        @pl.when(i == n_steps - 1)
        def _():
            _wait_record_gather(n_rec, y_hbm, nxt, nxt_sem)

    @pl.when(jnp.logical_not(odd))
    def _():
        accumulate(ybuf_even, sem.at[0], ybuf_odd, sem.at[1])

    @pl.when(odd)
    def _():
        accumulate(ybuf_odd, sem.at[1], ybuf_even, sem.at[0])

    for r in range(tmc // 8):
        rows = slice(r * 8, (r + 1) * 8)
        y = jnp.concatenate([yacc[c, rows, :] for c in range(n_col)], axis=1)
        xn = xres[rows, :] + gate[0] * ((y * _rms_scale(y)) * gpost[...])
        xout[rows, :] = xn
        if has_next:
            h = ((xn * _rms_scale(xn)) * gnext[...]) * (1.0 + sc[0]) + sh[0]
            for c in range(n_col):
                yacc[c, rows, :] = h[:, c * LANES:(c + 1) * LANES]
    if has_next:
        hout[...] = jnp.concatenate([yacc[c] for c in range(n_col)], axis=1).astype(hout.dtype)


def _moe_combine(y_rec, slot_tiles, wgt, hs, w_sd, xres, gate, gpost, nxt):
    m, d = xres.shape
    n_steps, _, n_rec = slot_tiles.shape
    tmc = n_rec // TOP_K
    f = hs.shape[1]
    has_next = nxt is not None
    slot_spec = lambda shift: pl.BlockSpec(
        (1, 1, n_rec), functools.partial(lambda i, shift: (jnp.minimum(i + shift, n_steps - 1), 0, 0), shift=shift),
        memory_space=pltpu.SMEM)
    vec_spec = pl.BlockSpec((1, d), lambda i: (0, 0))
    mod_spec = pl.BlockSpec((1, 1, d), lambda i: (i * tmc // SEQ, 0, 0))
    row_spec = pl.BlockSpec((tmc, d), lambda i: (i, 0))
    in_specs = [slot_spec(0), slot_spec(1), pl.BlockSpec(memory_space=pl.ANY),
                pl.BlockSpec((tmc, TOP_K), lambda i: (i, 0)), pl.BlockSpec((tmc, f), lambda i: (i, 0)),
                pl.BlockSpec((f, d), lambda i: (0, 0)), row_spec, mod_spec, vec_spec]
    args = [slot_tiles, slot_tiles, y_rec, wgt, hs, w_sd, xres, gate, gpost.reshape(1, d)]
    out_specs = [row_spec]
    out_shape = [jax.ShapeDtypeStruct((m, d), F32)]
    if has_next:
        gnext, sh, sc = nxt
        in_specs += [vec_spec, mod_spec, mod_spec]
        args += [gnext.reshape(1, d), sh, sc]
        out_specs.append(row_spec)
        out_shape.append(jax.ShapeDtypeStruct((m, d), BF16))
    outs = pl.pallas_call(
        functools.partial(_moe_combine_kernel, has_next=has_next, n_steps=n_steps),
        grid=(n_steps,),
        in_specs=in_specs,
        out_specs=out_specs,
        out_shape=out_shape,
        scratch_shapes=[pltpu.VMEM((n_rec * RECORD_PITCH, LANES), jnp.int32),
                        pltpu.VMEM((n_rec * RECORD_PITCH, LANES), jnp.int32), pltpu.SemaphoreType.DMA((2,)),
                        pltpu.VMEM((d // LANES, tmc, LANES), F32), pltpu.VMEM((TOP_K, tmc, LANES), F32)],
        compiler_params=_cparams(("arbitrary",), 48),
        name="moe_combine",
    )(*args)
    return (outs[0], outs[1]) if has_next else (outs[0], None)


def _mod_parts(mods_layer, n_rows):
    return [mods_layer[:n_rows, j * D_MODEL:(j + 1) * D_MODEL].reshape(n_rows, 1, D_MODEL) for j in range(6)]


def _moe(h2, h2_rec, x1, gate2, gpost, nxt, router_w, router_bias, layer, w_gate, w_up, w_down, sh_gate, sh_up, sh_down):
    n = h2.shape[0]
    tm, tmc = MOE_TILE, COMBINE_TOKENS
    t_max = n * TOP_K // tm + N_EXPERTS
    idx, wgt, pos, counts, hs = _router(h2, router_w, router_bias, sh_gate, sh_up)
    tiles = (counts[:, 0] + tm - 1) // tm
    tile_end = jnp.cumsum(tiles)
    n_tiles = tile_end[-1]
    first_slot = (tile_end - tiles) * tm
    slot = jnp.sum(jnp.where(idx[:, :, None] == jnp.arange(N_EXPERTS)[None, None, :], first_slot[None, None, :], 0),
                   axis=-1) + pos
    tok = jnp.broadcast_to(jnp.arange(n, dtype=jnp.int32)[None, :], slot.shape)
    tok_slot = jnp.zeros((t_max * tm,), jnp.int32).at[slot.reshape(-1)].set(tok.reshape(-1), unique_indices=True)
    tile_id = jnp.minimum(jnp.arange(t_max), n_tiles - 1)
    tile_expert = jnp.sum(tile_id[:, None] >= tile_end[None, :], axis=1).astype(jnp.int32)
    y_rec = _moe_experts(h2_rec, tile_expert, tok_slot.reshape(t_max, 1, tm), layer, w_gate, w_up, w_down)
    slot_tiles = slot.reshape(TOP_K, n // tmc, tmc).transpose(1, 0, 2).reshape(n // tmc, 1, TOP_K * tmc)
    return _moe_combine(y_rec, slot_tiles, wgt.T, hs, sh_down.astype(BF16), x1, gate2, gpost, nxt)


def kernel(x, c, ctx, c_ctx, ada_w, ada_b, norm_g, attn_w_in, attn_w_out, na_rpb, q_norm_g, k_norm_g, mix_w_in, mix_w_out, conv_w, sgu_w, sgu_b, sgu_norm_g, router_w, router_bias, moe_w_gate, moe_w_up, moe_w_down, shared_w_gate, shared_w_up, shared_w_down):
    nb, s, d = x.shape
    assert (s, d) == (SEQ, D_MODEL) and ctx.shape == (nb, CTX_LEN, d)
    x_lat = x.reshape(nb * s, d)
    x_ctx = ctx.reshape(nb * CTX_LEN, d)

    c_rows = jnp.concatenate([c, c_ctx[None, :], jnp.zeros((16 - nb - 1, d), F32)], axis=0)
    mods = _adaln(c_rows, ada_w, ada_b)
    lat_mods = [_mod_parts(mods[i], nb) for i in range(DEPTH)]
    ctx_mods = _mod_parts(mods[0, nb:nb + 1], 1)

    cos, sin = _rope_tables()
    bands, pats, na_idx = _na_static_tables()

    sh1, sc1 = lat_mods[0][0], lat_mods[0][1]
    h_lat = _norm_mod(x_lat, norm_g[0, 0], sh1, sc1, SEQ)

    for i in range(DEPTH):
        _, _, g1, sh2, sc2, g2 = lat_mods[i]
        if i % 2 == 0:
            e = i // 2
            w_in = attn_w_in[e]
            kv_lo, kv_hi = NA_DIM, 3 * NA_DIM
            gk_lo = 3 * NA_DIM + GQA_Q_DIM
            h_ctx = _norm_mod(x_ctx, norm_g[i, 0], ctx_mods[0], ctx_mods[1], nb * CTX_LEN)
            w_ctx = jnp.concatenate([w_in[:, kv_lo:kv_hi], w_in[:, gk_lo:]], axis=1).astype(BF16)
            p_lat = _matmul(h_lat, w_in.astype(BF16))
            p_ctx = _matmul(h_ctx, w_ctx)
            blk = 4 * HEAD_DIM
            qg = _qk_prep(p_lat, kv_hi // blk, GQA_Q_DIM // blk, q_norm_g[e], cos, sin, True)
            kg = _qk_prep(p_lat, gk_lo // blk, GQA_KV_DIM // blk, k_norm_g[e], cos, sin, True)
            kcg = _qk_prep(p_ctx, 2 * NA_DIM // blk, GQA_KV_DIM // blk, k_norm_g[e], cos, sin, False)
            bias = _na_bias_table(na_rpb[e], na_idx)
            o_na = _na_attention(p_lat, p_ctx, bias, bands, pats)
            o_gqa = _gqa_attention(qg, kg, p_lat, kcg, p_ctx,
                                   (gk_lo + GQA_KV_DIM) // HEAD_DIM, (2 * NA_DIM + GQA_KV_DIM) // HEAD_DIM)
            parts = [o_na, o_gqa]
            w_out = attn_w_out[e].astype(BF16)
        else:
            o = i // 2
            p_mix = _matmul(h_lat, mix_w_in[o].astype(BF16))
            parts = [_conv_sgu(p_mix, conv_w[o], sgu_w[o], sgu_b[o], sgu_norm_g[o])]
            wo = mix_w_out[o]
            w_out = jnp.concatenate(
                [wo[:CONV_DIM].reshape(SGU_GROUPS, SGU_CH, d), wo[CONV_DIM:].reshape(SGU_GROUPS, SGU_CH, d)],
                axis=1).reshape(CONV_DIM + SGU_DIM, d).astype(BF16)
        x1, h2, h2_rec = _matmul_post(parts, w_out, x_lat, g1, norm_g[i, 1], norm_g[i, 2], sh2, sc2)
        nxt = None
        if i + 1 < DEPTH:
            nxt = (norm_g[i + 1, 0], lat_mods[i + 1][0], lat_mods[i + 1][1])
        x_lat, h_lat = _moe(h2, h2_rec, x1, g2, norm_g[i, 3], nxt, router_w[i], router_bias[i],
                            i, moe_w_gate, moe_w_up, moe_w_down,
                            shared_w_gate[i], shared_w_up[i], shared_w_down[i])
    return x_lat.reshape(nb, s, d)
```

```python
import functools

import numpy as np
import jax
import jax.numpy as jnp
from jax import lax
from jax.experimental import pallas as pl
from jax.experimental.pallas import tpu as pltpu

F32 = jnp.float32
BF16 = jnp.bfloat16

D_MODEL = 4096
SEQ = 2048
DEPTH = 2
GRID_W = 64
GRID_ROWS = SEQ // GRID_W
CTX_LEN = 256
HEAD_DIM = 128
N_NA_HEADS = 16
N_GQA_HEADS = 16
N_GQA_KV = 4
GQA_REP = N_GQA_HEADS // N_GQA_KV
NA_KH = 8
NA_KW = 16
NA_BAND_ROWS = NA_KH + 1
NA_KEYS = NA_BAND_ROWS * GRID_W
NA_QBLK = 2 * GRID_W
NA_NBLK = GRID_ROWS // 2
ROPE_THETA = 10000.0
CONV_DIM = D_MODEL // 2
SGU_DIM = D_MODEL // 2
SGU_GROUPS = 8
SGU_CH = SGU_DIM // SGU_GROUPS
SGU_CHUNK = 128
N_EXPERTS = 64
N_EXPERT_GROUPS = 8
EXPERTS_PER_GROUP = N_EXPERTS // N_EXPERT_GROUPS
TOPK_GROUPS = 4
TOP_K = 8
D_EXPERT = D_MODEL // 16
ROUTED_SCALE = 2.5
EPS = 1e-6
NEG = -1e30
NA_DIM = N_NA_HEADS * HEAD_DIM
GQA_Q_DIM = N_GQA_HEADS * HEAD_DIM
GQA_KV_DIM = N_GQA_KV * HEAD_DIM
ATTN_SCALE = HEAD_DIM ** -0.5
LANES = 128
REC_ROWS = D_MODEL // 2 // LANES
MOE_TILE = 256
RECORD_PITCH = 20
COMBINE_TOKENS = 64
EXPERT_RING = 3
EXPERT_GATHER_QUEUES = (1,)
COMBINE_GATHER_QUEUES = (0, 1)

V7X_VMEM_BYTES = 64 * 1024 * 1024
MIB = 1024 * 1024


def _cparams(semantics, vmem_mib):
    assert vmem_mib * MIB < V7X_VMEM_BYTES
    return pltpu.CompilerParams(dimension_semantics=semantics, vmem_limit_bytes=vmem_mib * MIB)


def _rms_scale(x):
    return lax.rsqrt(jnp.mean(x * x, axis=-1, keepdims=True) + EPS)


def _silu(x):
    return x / (1.0 + jnp.exp(-x))


def _adaln_kernel(c_ref, w_ref, b_ref, o_ref):
    a = _silu(c_ref[...])
    o_ref[0] = jnp.dot(a.astype(BF16), w_ref[0].astype(BF16), preferred_element_type=F32) + b_ref[0]


def _adaln(c_rows, ada_w, ada_b):
    depth, d, n6 = ada_w.shape
    rows = c_rows.shape[0]
    tn = 512
    return pl.pallas_call(
        _adaln_kernel,
        grid=(depth, n6 // tn),
        in_specs=[pl.BlockSpec((rows, d), lambda l, j: (0, 0)),
                  pl.BlockSpec((1, d, tn), lambda l, j: (l, 0, j)),
                  pl.BlockSpec((1, 1, tn), lambda l, j: (l, 0, j))],
        out_specs=pl.BlockSpec((1, rows, tn), lambda l, j: (l, 0, j)),
        out_shape=jax.ShapeDtypeStruct((depth, rows, n6), F32),
        compiler_params=_cparams(("parallel", "parallel"), 40),
        name="adaln",
    )(c_rows, ada_w, ada_b.reshape(depth, 1, n6))


def _norm_mod_kernel(x_ref, g_ref, sh_ref, sc_ref, o_ref):
    x = x_ref[...]
    y = (x * _rms_scale(x)) * g_ref[...]
    o_ref[...] = (y * (1.0 + sc_ref[0]) + sh_ref[0]).astype(o_ref.dtype)


def _norm_mod(x, g, sh, sc, rows_per_mod):
    m, d = x.shape
    tm = 256
    mod_spec = pl.BlockSpec((1, 1, d), lambda i: (i * tm // rows_per_mod, 0, 0))
    return pl.pallas_call(
        _norm_mod_kernel,
        grid=(m // tm,),
        in_specs=[pl.BlockSpec((tm, d), lambda i: (i, 0)),
                  pl.BlockSpec((1, d), lambda i: (0, 0)),
                  mod_spec, mod_spec],
        out_specs=pl.BlockSpec((tm, d), lambda i: (i, 0)),
        out_shape=jax.ShapeDtypeStruct((m, d), BF16),
        compiler_params=_cparams(("parallel",), 40),
        name="norm_mod",
    )(x, g.reshape(1, d), sh, sc)


def _mm_kernel(x_ref, w_ref, o_ref):
    o_ref[...] = jnp.dot(x_ref[...], w_ref[...], preferred_element_type=F32).astype(o_ref.dtype)


def _matmul(x, w, tm=1024, tn=512):
    m, k = x.shape
    n = w.shape[1]
    tm = min(tm, m)
    return pl.pallas_call(
        _mm_kernel,
        grid=(m // tm, n // tn),
        in_specs=[pl.BlockSpec((tm, k), lambda i, j: (i, 0)),
                  pl.BlockSpec((k, tn), lambda i, j: (0, j))],
        out_specs=pl.BlockSpec((tm, tn), lambda i, j: (i, j)),
        out_shape=jax.ShapeDtypeStruct((m, n), BF16),
        compiler_params=_cparams(("parallel", "arbitrary"), 48),
        name="matmul",
    )(x, w)


def _pack_bf16_pairs(lo, hi):
    return pltpu.pack_elementwise([lo, hi], packed_dtype=BF16)


def _unpack_bf16_pairs(words):
    lo = pltpu.unpack_elementwise(words, index=0, packed_dtype=BF16, unpacked_dtype=F32)
    hi = pltpu.unpack_elementwise(words, index=1, packed_dtype=BF16, unpacked_dtype=F32)
    return lo, hi


def _store_records(ref, first_row, words, pitch):
    n = words.shape[0]
    for s in range(REC_ROWS):
        ref[pl.ds(first_row + s, n, stride=pitch), :] = words[:, s * LANES:(s + 1) * LANES]
    for s in range(REC_ROWS, pitch):
        ref[pl.ds(first_row + s, n, stride=pitch), :] = jnp.zeros((n, LANES), words.dtype)


def _load_records(ref, first_row, n, pitch):
    return jnp.concatenate([ref[pl.ds(first_row + s, n, stride=pitch), :] for s in range(REC_ROWS)], axis=1)


def _mm_post_kernel(*refs, n_parts, nk, n_chunks, row_chunk):
    lhs = refs[:n_parts]
    ws = refs[n_parts:2 * n_parts]
    xres, gate, gpost, gnext, sh, sc, xout, hout, hrec, vecs = refs[2 * n_parts:]
    k = pl.program_id(1)
    tm, d = xout.shape
    tn = d // n_chunks

    def part(c):
        acc = None
        for l, w in zip(lhs, ws):
            t = jnp.dot(l[...], w[:, c * tn:(c + 1) * tn], preferred_element_type=F32)
            acc = t if acc is None else acc + t
        return acc

    @pl.when(k == 0)
    def _():
        for c in range(n_chunks):
            xout[:, c * tn:(c + 1) * tn] = part(c)

    @pl.when(k > 0)
    def _():
        for c in range(n_chunks):
            xout[:, c * tn:(c + 1) * tn] += part(c)

    @pl.when(k == nk - 1)
    def _():
        vecs[0:1, :] = gate[0] * gpost[...]
        vecs[1:2, :] = gnext[...] * (1.0 + sc[0])

        def body(r, carry):
            rows = pl.ds(pl.multiple_of(r * row_chunk, row_chunk), row_chunk)
            y = xout[rows, :]
            xn = xres[rows, :] + (y * _rms_scale(y)) * vecs[0:1, :]
            xout[rows, :] = xn
            h = (xn * _rms_scale(xn)) * vecs[1:2, :] + sh[0]
            hout[rows, :] = h.astype(hout.dtype)
            half = d // 2
            first = pl.multiple_of(r * (row_chunk * RECORD_PITCH), row_chunk * RECORD_PITCH)
            _store_records(hrec, first, _pack_bf16_pairs(h[:, :half], h[:, half:]), RECORD_PITCH)
            return carry
        lax.fori_loop(0, tm // row_chunk, body, 0)


def _matmul_post(lhs_parts, w, xres, gate, gpost, gnext, sh, sc, tm=512):
    m, d = xres.shape
    n_parts = len(lhs_parts)
    kp = lhs_parts[0].shape[1]
    assert all(l.shape == (m, kp) for l in lhs_parts) and w.shape == (n_parts * kp, d)
    tk = 512 // n_parts
    nk = kp // tk
    vec_spec = pl.BlockSpec((1, d), lambda i, k: (0, 0))
    mod_spec = pl.BlockSpec((1, 1, d), lambda i, k: (i * tm // SEQ, 0, 0))
    row_spec = pl.BlockSpec((tm, d), lambda i, k: (i, 0))
    in_specs = [pl.BlockSpec((tm, tk), lambda i, k: (i, k)) for _ in range(n_parts)]
    in_specs += [pl.BlockSpec((tk, d), functools.partial(lambda i, k, off: (k + off, 0), off=p * nk))
                 for p in range(n_parts)]
    in_specs += [pl.BlockSpec((tm, d), lambda i, k: (i, 0), pipeline_mode=pl.Buffered(1)),
                 mod_spec, vec_spec, vec_spec, mod_spec, mod_spec]
    args = list(lhs_parts) + [w] * n_parts + [xres, gate, gpost.reshape(1, d), gnext.reshape(1, d), sh, sc]
    return pl.pallas_call(
        functools.partial(_mm_post_kernel, n_parts=n_parts, nk=nk, n_chunks=4, row_chunk=32),
        grid=(m // tm, nk),
        in_specs=in_specs,
        out_specs=[row_spec, row_spec, pl.BlockSpec((tm * RECORD_PITCH, LANES), lambda i, k: (i, 0))],
        out_shape=[jax.ShapeDtypeStruct((m, d), F32), jax.ShapeDtypeStruct((m, d), BF16),
                   jax.ShapeDtypeStruct((m * RECORD_PITCH, LANES), jnp.int32)],
        scratch_shapes=[pltpu.VMEM((8, d), F32)],
        compiler_params=_cparams(("parallel", "arbitrary"), 56),
        name="matmul_post",
    )(*args)


def _qk_prep_kernel(x_ref, g_ref, cos_ref, sin_ref, o_ref, *, rope):
    n_heads = x_ref.shape[1] // HEAD_DIM
    for hh in range(n_heads):
        cols = slice(hh * HEAD_DIM, (hh + 1) * HEAD_DIM)
        x = x_ref[:, cols].astype(F32)
        y = (x * _rms_scale(x)) * g_ref[...]
        if rope:
            lane = lax.broadcasted_iota(jnp.int32, y.shape, 1)
            quarter = HEAD_DIM // 4
            partner = jnp.where(jnp.bitwise_and(lane, 2 * quarter - 1) < quarter,
                                pltpu.roll(y, HEAD_DIM - quarter, axis=1), pltpu.roll(y, quarter, axis=1))
            y = y * cos_ref[...] + partner * sin_ref[...]
        o_ref[:, cols] = y.astype(o_ref.dtype)


def _qk_prep(p, col_block0, n_col_blocks, g, cos, sin, rope):
    m = p.shape[0]
    tm, tc = 512, 4 * HEAD_DIM
    pos_blocks = SEQ // tm
    return pl.pallas_call(
        functools.partial(_qk_prep_kernel, rope=rope),
        grid=(m // tm, n_col_blocks),
        in_specs=[pl.BlockSpec((tm, tc), lambda i, j: (i, col_block0 + j)),
                  pl.BlockSpec((1, HEAD_DIM), lambda i, j: (0, 0)),
                  pl.BlockSpec((tm, HEAD_DIM), lambda i, j: (i % pos_blocks, 0)),
                  pl.BlockSpec((tm, HEAD_DIM), lambda i, j: (i % pos_blocks, 0))],
        out_specs=pl.BlockSpec((tm, tc), lambda i, j: (i, j)),
        out_shape=jax.ShapeDtypeStruct((m, tc * n_col_blocks), BF16),
        compiler_params=_cparams(("parallel", "parallel"), 32),
        name="qk_prep",
    )(p, g.reshape(1, HEAD_DIM), cos, sin)


def _rope_tables():
    half = HEAD_DIM // 4
    t = jnp.arange(SEQ)
    freqs = ROPE_THETA ** (-jnp.arange(half, dtype=F32) / half)
    ang_r = (t // GRID_W).astype(F32)[:, None] * freqs[None, :]
    ang_c = (t % GRID_W).astype(F32)[:, None] * freqs[None, :]
    cos = jnp.concatenate([jnp.cos(ang_r)] * 2 + [jnp.cos(ang_c)] * 2, axis=-1)
    sin = jnp.concatenate([-jnp.sin(ang_r), jnp.sin(ang_r), -jnp.sin(ang_c), jnp.sin(ang_c)], axis=-1)
    return cos, sin


def _na_static_tables():
    rows, kh, kw = GRID_ROWS, min(NA_KH, GRID_ROWS), NA_KW
    nb_rows = min(kh + 1, rows)
    assert nb_rows == NA_BAND_ROWS
    col = np.arange(GRID_W)
    col_start = np.clip(col - kw // 2, 0, GRID_W - kw)
    patterns, pat_of_block, bands = [], [], []
    for b in range(NA_NBLK):
        q_rows = 2 * b + np.arange(2)
        row_start = np.clip(q_rows - kh // 2, 0, rows - kh)
        band = int(np.clip(row_start[0], 0, rows - nb_rows))
        key = (tuple(q_rows - band), tuple(row_start - band))
        if key not in patterns:
            patterns.append(key)
        pat_of_block.append(patterns.index(key))
        bands.append(band)
    idx = np.full((len(patterns), NA_QBLK, NA_KEYS), -1, np.int32)
    for p, (q_rel, rs_rel) in enumerate(patterns):
        qr = np.repeat(np.array(q_rel), GRID_W)
        qc = np.tile(col, 2)
        rs = np.repeat(np.array(rs_rel), GRID_W)
        cs = np.tile(col_start, 2)
        kr = np.repeat(np.arange(nb_rows), GRID_W)
        kc = np.tile(col, nb_rows)
        in_win = ((kr[None, :] >= rs[:, None]) & (kr[None, :] < rs[:, None] + kh)
                  & (kc[None, :] >= cs[:, None]) & (kc[None, :] < cs[:, None] + kw))
        dr = np.clip(kr[None, :] - qr[:, None] + NA_KH - 1, 0, 2 * NA_KH - 2)
        dc = np.clip(kc[None, :] - qc[:, None] + NA_KW - 1, 0, 2 * NA_KW - 2)
        idx[p] = np.where(in_win, dr * (2 * NA_KW - 1) + dc, -1)
    return np.array(bands, np.int32), np.array(pat_of_block, np.int32), idx


def _na_bias_kernel(idx_ref, rpb_ref, o_ref):
    idx = idx_ref[0]
    n_tab = rpb_ref.shape[1]
    onehot = (lax.broadcasted_iota(jnp.int32, (n_tab, idx.shape[1]), 0) == idx).astype(BF16)
    r = rpb_ref[...]
    hi = r.astype(BF16)
    r1 = r - hi.astype(F32)
    mid = r1.astype(BF16)
    lo = (r1 - mid.astype(F32)).astype(BF16)
    val = (jnp.dot(hi, onehot, preferred_element_type=F32) + jnp.dot(mid, onehot, preferred_element_type=F32)
           + jnp.dot(lo, onehot, preferred_element_type=F32))
    o_ref[0] = jnp.where(idx < 0, NEG, val)


def _na_bias_table(rpb, idx):
    n_pat = idx.shape[0]
    h = rpb.shape[0]
    n_tab = 512
    flat = NA_QBLK * NA_KEYS
    tile = 8 * NA_KEYS
    rpb_flat = jnp.pad(rpb.reshape(h, -1), ((0, 0), (0, n_tab - rpb.shape[1] * rpb.shape[2])))
    out = pl.pallas_call(
        _na_bias_kernel,
        grid=(n_pat, flat // tile),
        in_specs=[pl.BlockSpec((1, 1, tile), lambda p, j: (p, 0, j)),
                  pl.BlockSpec((h, n_tab), lambda p, j: (0, 0))],
        out_specs=pl.BlockSpec((1, h, tile), lambda p, j: (p, 0, j)),
        out_shape=jax.ShapeDtypeStruct((n_pat, h, flat), F32),
        compiler_params=_cparams(("parallel", "parallel"), 32),
        name="na_bias",
    )(jnp.asarray(idx.reshape(n_pat, 1, flat)), rpb_flat)
    return out.reshape(n_pat, h, NA_QBLK, NA_KEYS)


_NT = (((1,), (1,)), ((), ()))


def _na_attn_kernel(q_ref, k_ref, v_ref, kc_ref, vc_ref, bias_ref, o_ref, *, bands, pats):
    kc = kc_ref[...]
    vc = vc_ref[...]

    for n in range(NA_NBLK):
        q_rows = slice(n * NA_QBLK, (n + 1) * NA_QBLK)
        key_rows = slice(bands[n] * GRID_W, bands[n] * GRID_W + NA_KEYS)
        q = q_ref[q_rows, :]
        bias = bias_ref[pats[n], 0]
        s = lax.dot_general(q, k_ref[key_rows, :], _NT, preferred_element_type=F32) * ATTN_SCALE + bias
        s = jnp.where(bias <= 0.5 * NEG, NEG, s)
        s_ctx = lax.dot_general(q, kc, _NT, preferred_element_type=F32) * ATTN_SCALE
        m = jnp.maximum(jnp.max(s, axis=-1, keepdims=True), jnp.max(s_ctx, axis=-1, keepdims=True))
        p = jnp.exp(s - m)
        p_ctx = jnp.exp(s_ctx - m)
        denom = jnp.sum(p, axis=-1, keepdims=True) + jnp.sum(p_ctx, axis=-1, keepdims=True)
        o = (jnp.dot(p.astype(BF16), v_ref[key_rows, :], preferred_element_type=F32)
             + jnp.dot(p_ctx.astype(BF16), vc, preferred_element_type=F32))
        o_ref[q_rows, :] = (o / denom).astype(o_ref.dtype)


def _na_attention(p_lat, p_ctx, bias, bands, pats):
    n_tok = p_lat.shape[0]
    nb = n_tok // SEQ
    n_pat = bias.shape[0]
    hd = HEAD_DIM
    return pl.pallas_call(
        functools.partial(_na_attn_kernel, bands=tuple(int(b) for b in bands), pats=tuple(int(p) for p in pats)),
        grid=(N_NA_HEADS, nb),
        in_specs=[pl.BlockSpec((SEQ, hd), lambda h, b: (b, h)),
                  pl.BlockSpec((SEQ, hd), lambda h, b: (b, N_NA_HEADS + h)),
                  pl.BlockSpec((SEQ, hd), lambda h, b: (b, 2 * N_NA_HEADS + h)),
                  pl.BlockSpec((CTX_LEN, hd), lambda h, b: (b, h)),
                  pl.BlockSpec((CTX_LEN, hd), lambda h, b: (b, N_NA_HEADS + h)),
                  pl.BlockSpec((n_pat, 1, NA_QBLK, NA_KEYS), lambda h, b: (0, h, 0, 0))],
        out_specs=pl.BlockSpec((SEQ, hd), lambda h, b: (b, h)),
        out_shape=jax.ShapeDtypeStruct((n_tok, NA_DIM), BF16),
        compiler_params=_cparams(("parallel", "parallel"), 32),
        name="na_attention",
    )(p_lat, p_lat, p_lat, p_ctx, p_ctx, bias)


def _gqa_attn_kernel(q_ref, k_ref, v_ref, kc_ref, vc_ref, o_ref):
    k = k_ref[...]
    v = v_ref[...]
    kc = kc_ref[...]
    vc = vc_ref[...]
    for r in range(GQA_REP):
        cols = slice(r * HEAD_DIM, (r + 1) * HEAD_DIM)
        q = q_ref[:, cols]
        s = lax.dot_general(q, k, _NT, preferred_element_type=F32) * ATTN_SCALE
        s_ctx = lax.dot_general(q, kc, _NT, preferred_element_type=F32) * ATTN_SCALE
        m = jnp.maximum(jnp.max(s, axis=-1, keepdims=True), jnp.max(s_ctx, axis=-1, keepdims=True))
        p = jnp.exp(s - m)
        p_ctx = jnp.exp(s_ctx - m)
        denom = jnp.sum(p, axis=-1, keepdims=True) + jnp.sum(p_ctx, axis=-1, keepdims=True)
        o = (jnp.dot(p.astype(BF16), v, preferred_element_type=F32)
             + jnp.dot(p_ctx.astype(BF16), vc, preferred_element_type=F32))
        o_ref[:, cols] = (o / denom).astype(o_ref.dtype)


def _gqa_attention(qg, kg, p_lat, kcg, p_ctx, v_col_block, vc_col_block, tq=256):
    n_tok = qg.shape[0]
    nb = n_tok // SEQ
    hd = HEAD_DIM
    qt = SEQ // tq
    return pl.pallas_call(
        _gqa_attn_kernel,
        grid=(nb, N_GQA_KV, qt),
        in_specs=[pl.BlockSpec((tq, GQA_REP * hd), lambda b, g, t: (b * qt + t, g)),
                  pl.BlockSpec((SEQ, hd), lambda b, g, t: (b, g)),
                  pl.BlockSpec((SEQ, hd), lambda b, g, t: (b, v_col_block + g)),
                  pl.BlockSpec((CTX_LEN, hd), lambda b, g, t: (b, g)),
                  pl.BlockSpec((CTX_LEN, hd), lambda b, g, t: (b, vc_col_block + g))],
        out_specs=pl.BlockSpec((tq, GQA_REP * hd), lambda b, g, t: (b * qt + t, g)),
        out_shape=jax.ShapeDtypeStruct((n_tok, GQA_Q_DIM), BF16),
        compiler_params=_cparams(("parallel", "parallel", "parallel"), 48),
        name="gqa_attention",
    )(qg, kg, p_lat, kcg, p_ctx)


def _mixer_kernel(gb_ref, gc_ref, xc_ref, u_ref, v_ref, cw_ref, sw_ref, sb_ref, sg_ref, o_ref):
    s, c = gb_ref.shape
    z = gc_ref[...].astype(F32) * xc_ref[...].astype(F32)
    row = lax.broadcasted_iota(jnp.int32, (s, c), 0)
    z_prev = jnp.where(row == 0, 0.0, pltpu.roll(z, 1, axis=0))
    z_next = jnp.where(row == s - 1, 0.0, pltpu.roll(z, s - 1, axis=0))
    cw = cw_ref[...]
    conv = z_prev * cw[0:1, :] + z * cw[1:2, :] + z_next * cw[2:3, :]
    o_ref[:, :c] = (gb_ref[...].astype(F32) * conv).astype(o_ref.dtype)
    w = sw_ref[0].astype(BF16)
    for n in range(s // SGU_CHUNK):
        rows = slice(n * SGU_CHUNK, (n + 1) * SGU_CHUNK)
        vv = v_ref[rows, :].astype(F32)
        vg = (vv * _rms_scale(vv)) * sg_ref[0]
        mix = jnp.dot(w, vg.astype(BF16), preferred_element_type=F32) + sb_ref[0]
        o_ref[rows, c:] = (u_ref[rows, :].astype(F32) * mix).astype(o_ref.dtype)


def _conv_sgu(p, conv_w, sgu_w, sgu_b, sgu_g):
    n_tok = p.shape[0]
    nb = n_tok // SEQ
    c = SGU_CH
    blocks = CONV_DIM // c
    col = lambda off: pl.BlockSpec((SEQ, c), functools.partial(lambda b, g, off: (b, off + g), off=off))
    return pl.pallas_call(
        _mixer_kernel,
        grid=(nb, SGU_GROUPS),
        in_specs=[col(0), col(blocks), col(2 * blocks), col(3 * blocks), col(4 * blocks),
                  pl.BlockSpec((3, c), lambda b, g: (0, g)),
                  pl.BlockSpec((1, SGU_CHUNK, SGU_CHUNK), lambda b, g: (g, 0, 0)),
                  pl.BlockSpec((1, SGU_CHUNK, 1), lambda b, g: (g, 0, 0)),
                  pl.BlockSpec((1, 1, c), lambda b, g: (g, 0, 0))],
        out_specs=pl.BlockSpec((SEQ, 2 * c), lambda b, g: (b, g)),
        out_shape=jax.ShapeDtypeStruct((n_tok, CONV_DIM + SGU_DIM), BF16),
        compiler_params=_cparams(("parallel", "parallel"), 48),
        name="conv_sgu",
    )(p, p, p, p, p, conv_w, sgu_w, sgu_b.reshape(SGU_GROUPS, SGU_CHUNK, 1), sgu_g.reshape(SGU_GROUPS, 1, c))


def _router_kernel(x_ref, whi_ref, wlo_ref, bias_ref, tri_ref, wsh_ref,
                   idx_ref, wgt_ref, pos_ref, cnt_ref, hs_ref, carry_ref):
    @pl.when(pl.program_id(0) == 0)
    def _():
        carry_ref[...] = jnp.zeros_like(carry_ref)

    x = x_ref[...]
    hsu = jnp.dot(x, wsh_ref[...], preferred_element_type=F32)
    f = hsu.shape[1] // 2
    hs_ref[...] = (_silu(hsu[:, :f]) * hsu[:, f:]).astype(hs_ref.dtype)

    logits = (lax.dot_general(whi_ref[...], x, _NT, preferred_element_type=F32)
              + lax.dot_general(wlo_ref[...], x, _NT, preferred_element_type=F32))
    scores = 1.0 / (1.0 + jnp.exp(-logits))
    sel = scores + bias_ref[...]
    n_e, tm = sel.shape
    epg = EXPERTS_PER_GROUP
    groups = [sel[g * epg:(g + 1) * epg, :] for g in range(N_EXPERT_GROUPS)]
    gscore = []
    for sg in groups:
        top1 = jnp.max(sg, axis=0, keepdims=True)
        is_top = sg == top1
        n_top = jnp.sum(is_top.astype(F32), axis=0, keepdims=True)
        second = jnp.max(jnp.where(is_top, -jnp.inf, sg), axis=0, keepdims=True)
        gscore.append(top1 + jnp.where(n_top >= 2.0, top1, second))
    masked = []
    for g in range(N_EXPERT_GROUPS):
        rank = jnp.zeros((1, tm), F32)
        for o in range(N_EXPERT_GROUPS):
            if o == g:
                continue
            beats = (gscore[o] >= gscore[g]) if o < g else (gscore[o] > gscore[g])
            rank = rank + beats.astype(F32)
        masked.append(jnp.where(rank < float(TOPK_GROUPS), groups[g], -jnp.inf))
    v = jnp.concatenate(masked, axis=0)
    row = lax.broadcasted_iota(jnp.int32, (n_e, tm), 0)
    chosen = jnp.zeros((n_e, tm), F32)
    picks, top_s = [], []
    for _ in range(TOP_K):
        best = jnp.max(v, axis=0, keepdims=True)
        first = jnp.min(jnp.where(v == best, row, n_e), axis=0, keepdims=True)
        pick = row == first
        picks.append(first)
        top_s.append(jnp.sum(jnp.where(pick, scores, 0.0), axis=0, keepdims=True))
        chosen = jnp.where(pick, 1.0, chosen)
        v = jnp.where(pick, -jnp.inf, v)
    total = top_s[0]
    for s in top_s[1:]:
        total = total + s
    running = jnp.dot(chosen.astype(BF16), tri_ref[...], preferred_element_type=F32) + carry_ref[...]
    pos = [jnp.sum(jnp.where(row == first, running - 1.0, 0.0), axis=0, keepdims=True) for first in picks]
    idx_ref[...] = jnp.concatenate(picks, axis=0)
    wgt_ref[...] = jnp.concatenate([s / total * ROUTED_SCALE for s in top_s], axis=0)
    pos_ref[...] = jnp.concatenate(pos, axis=0).astype(jnp.int32)
    carry_ref[...] += jnp.sum(chosen, axis=1, keepdims=True)
    cnt_ref[...] = carry_ref[...].astype(jnp.int32)


def _router(h, router_w, router_bias, sh_gate, sh_up, tm=1024):
    n, d = h.shape
    f = sh_gate.shape[1]
    wt = router_w.T
    w_hi = wt.astype(BF16)
    w_lo = (wt - w_hi.astype(F32)).astype(BF16)
    tri = (jnp.arange(tm)[:, None] <= jnp.arange(tm)[None, :]).astype(BF16)
    w_sh = jnp.concatenate([sh_gate, sh_up], axis=1).astype(BF16)
    fixed = lambda shape: pl.BlockSpec(shape, lambda i: (0, 0))
    per_tok = pl.BlockSpec((TOP_K, tm), lambda i: (0, i))
    return pl.pallas_call(
        _router_kernel,
        grid=(n // tm,),
        in_specs=[pl.BlockSpec((tm, d), lambda i: (i, 0)), fixed((N_EXPERTS, d)), fixed((N_EXPERTS, d)),
                  fixed((N_EXPERTS, 1)), fixed((tm, tm)), fixed((d, 2 * f))],
        out_specs=[per_tok, per_tok, per_tok, fixed((N_EXPERTS, 1)), pl.BlockSpec((tm, f), lambda i: (i, 0))],
        out_shape=[jax.ShapeDtypeStruct((TOP_K, n), jnp.int32), jax.ShapeDtypeStruct((TOP_K, n), F32),
                   jax.ShapeDtypeStruct((TOP_K, n), jnp.int32), jax.ShapeDtypeStruct((N_EXPERTS, 1), jnp.int32),
                   jax.ShapeDtypeStruct((n, f), BF16)],
        scratch_shapes=[pltpu.VMEM((N_EXPERTS, 1), F32)],
        compiler_params=_cparams(("arbitrary",), 48),
        name="router",
    )(h, w_hi, w_lo, router_bias.reshape(N_EXPERTS, 1), tri, w_sh)


def _start_record_gather(idx_ref, n, src_hbm, dst, sem, pitch, priorities, unroll=8):
    def body(j, carry):
        for u in range(unroll):
            r = j * unroll + u
            src_row = pl.multiple_of(idx_ref[0, 0, r] * pitch, 4)
            dst_row = pl.multiple_of(r * pitch, 4)
            pltpu.make_async_copy(src_hbm.at[pl.ds(src_row, REC_ROWS), :], dst.at[pl.ds(dst_row, REC_ROWS), :],
                                  sem).start(priority=priorities[u % len(priorities)])
        return carry
    lax.fori_loop(0, n // unroll, body, 0)


def _wait_record_gather(n, src_hbm, dst, sem):
    pltpu.make_async_copy(src_hbm.at[pl.ds(0, n * REC_ROWS), :], dst.at[pl.ds(0, n * REC_ROWS), :], sem).wait()


def _issue_records(idx_ref, r0, r1, src_hbm, dst, sem, pitch, priorities):
    for r in range(r0, r1):
        src_row = pl.multiple_of(idx_ref[0, 0, r] * pitch, 4)
        pltpu.make_async_copy(src_hbm.at[pl.ds(src_row, REC_ROWS), :], dst.at[pl.ds(r * pitch, REC_ROWS), :],
                              sem).start(priority=priorities[r % len(priorities)])


def _moe_expert_kernel(te_ref, tok_ref, tok_next_ref, tok_ahead_ref, x_hbm, wg_ref, wu_ref, wd_ref, y_ref,
                       xbuf, sem, wg_bf, wu_bf, wd_bf):
    i = pl.program_id(0)
    last = pl.num_programs(0) - 1
    slot = lax.rem(i, EXPERT_RING)
    slot_next = lax.rem(i + 1, EXPERT_RING)
    slot_ahead = lax.rem(i + 2, EXPERT_RING)
    tm = tok_ref.shape[2]
    half = D_MODEL // 2
    chunk = 4 * LANES
    n_chunks = half // chunk

    @pl.when(i == 0)
    def _():
        _start_record_gather(tok_ref, tm, x_hbm, xbuf.at[0], sem.at[0], RECORD_PITCH, EXPERT_GATHER_QUEUES)
        _start_record_gather(tok_next_ref, tm, x_hbm, xbuf.at[1], sem.at[1], RECORD_PITCH, EXPERT_GATHER_QUEUES)

    @pl.when(jnp.logical_or(i == 0, te_ref[i] != te_ref[jnp.maximum(i - 1, 0)]))
    def _():
        wg_bf[...] = wg_ref[0, 0].astype(BF16)
        wu_bf[...] = wu_ref[0, 0].astype(BF16)
        wd_bf[...] = wd_ref[0, 0].astype(BF16)

    _wait_record_gather(tm, x_hbm, xbuf.at[slot], sem.at[slot])
    lo, hi = _unpack_bf16_pairs(_load_records(xbuf.at[slot], 0, tm, RECORD_PITCH))
    lo = lo.astype(BF16)
    hi = hi.astype(BF16)
    n_groups = n_chunks + 2
    bounds = [tm * g // n_groups for g in range(n_groups + 1)]
    prefetch = lambda g: _issue_records(tok_ahead_ref, bounds[g], bounds[g + 1], x_hbm, xbuf.at[slot_ahead],
                                        sem.at[slot_ahead], RECORD_PITCH, EXPERT_GATHER_QUEUES)
    prefetch(0)
    pre_g = (jnp.dot(lo, wg_bf[:half, :], preferred_element_type=F32)
             + jnp.dot(hi, wg_bf[half:, :], preferred_element_type=F32))
    prefetch(1)
    pre_u = (jnp.dot(lo, wu_bf[:half, :], preferred_element_type=F32)
             + jnp.dot(hi, wu_bf[half:, :], preferred_element_type=F32))
    hid = (_silu(pre_g) * pre_u).astype(BF16)
    for c in range(n_chunks):
        prefetch(2 + c)
        y_lo = jnp.dot(hid, wd_bf[:, c * chunk:(c + 1) * chunk], preferred_element_type=F32)
        y_hi = jnp.dot(hid, wd_bf[:, half + c * chunk:half + (c + 1) * chunk], preferred_element_type=F32)
        words = _pack_bf16_pairs(y_lo, y_hi)
        for s in range(chunk // LANES):
            y_ref[pl.ds(c * (chunk // LANES) + s, tm, stride=RECORD_PITCH), :] = words[:, s * LANES:(s + 1) * LANES]
    for s in range(REC_ROWS, RECORD_PITCH):
        y_ref[pl.ds(s, tm, stride=RECORD_PITCH), :] = jnp.zeros((tm, LANES), y_ref.dtype)

    @pl.when(i == last)
    def _():
        _wait_record_gather(tm, x_hbm, xbuf.at[slot_next], sem.at[slot_next])
        _wait_record_gather(tm, x_hbm, xbuf.at[slot_ahead], sem.at[slot_ahead])


def _moe_experts(x_rec, tile_expert, tok_slot, layer, w_gate, w_up, w_down):
    t_max, _, tm = tok_slot.shape
    _, e, d, f = w_gate.shape
    tok_spec = lambda shift: pl.BlockSpec(
        (1, 1, tm), functools.partial(lambda i, te, shift: (jnp.minimum(i + shift, t_max - 1), 0, 0), shift=shift),
        memory_space=pltpu.SMEM)
    grid_spec = pltpu.PrefetchScalarGridSpec(
        num_scalar_prefetch=1,
        grid=(t_max,),
        in_specs=[tok_spec(0), tok_spec(1), tok_spec(2),
                  pl.BlockSpec(memory_space=pl.ANY),
                  pl.BlockSpec((1, 1, d, f), lambda i, te: (layer, te[i], 0, 0)),
                  pl.BlockSpec((1, 1, d, f), lambda i, te: (layer, te[i], 0, 0)),
                  pl.BlockSpec((1, 1, f, d), lambda i, te: (layer, te[i], 0, 0))],
        out_specs=pl.BlockSpec((tm * RECORD_PITCH, LANES), lambda i, te: (i, 0)),
        scratch_shapes=[pltpu.VMEM((EXPERT_RING, tm * RECORD_PITCH, LANES), jnp.int32),
                        pltpu.SemaphoreType.DMA((EXPERT_RING,)),
                        pltpu.VMEM((d, f), BF16), pltpu.VMEM((d, f), BF16), pltpu.VMEM((f, d), BF16)],
    )
    return pl.pallas_call(
        _moe_expert_kernel,
        grid_spec=grid_spec,
        out_shape=jax.ShapeDtypeStruct((t_max * tm * RECORD_PITCH, LANES), jnp.int32),
        compiler_params=_cparams(("arbitrary",), 56),
        name="moe_experts",
    )(tile_expert, tok_slot, tok_slot, tok_slot, x_rec, w_gate, w_up, w_down)


def _moe_combine_kernel(*refs, has_next, n_steps):
    if has_next:
        (slot_ref, slot_next_ref, y_hbm, wgt_ref, hs_ref, wsd_ref, xres, gate, gpost, gnext, sh, sc,
         xout, hout, ybuf_even, ybuf_odd, sem, yacc, wbuf) = refs
    else:
        (slot_ref, slot_next_ref, y_hbm, wgt_ref, hs_ref, wsd_ref, xres, gate, gpost,
         xout, ybuf_even, ybuf_odd, sem, yacc, wbuf) = refs
    i = pl.program_id(0)
    odd = lax.rem(i, 2) == 1
    tmc = xout.shape[0]
    n_rec = TOP_K * tmc
    n_col = xout.shape[1] // LANES

    @pl.when(i == 0)
    def _():
        _start_record_gather(slot_ref, n_rec, y_hbm, ybuf_even, sem.at[0], RECORD_PITCH, COMBINE_GATHER_QUEUES)

    hs = hs_ref[...]
    for c in range(n_col):
        yacc[c] = jnp.dot(hs, wsd_ref[:, c * LANES:(c + 1) * LANES], preferred_element_type=F32)
    wgt = wgt_ref[...]
    for k in range(TOP_K):
        wbuf[k] = jnp.broadcast_to(wgt[:, k:k + 1], (tmc, LANES))
    n_iter = TOP_K // 2
    per_s = n_rec // (n_iter * REC_ROWS)

    def accumulate(cur, cur_sem, nxt, nxt_sem):
        _wait_record_gather(n_rec, y_hbm, cur, cur_sem)

        def body(kp, carry):
            ka = 2 * kp
            wa = wbuf[ka]
            wb = wbuf[ka + 1]
            first = ka * (tmc * RECORD_PITCH)
            for s in range(REC_ROWS):
                lo_a, hi_a = _unpack_bf16_pairs(cur[pl.ds(first + s, tmc, stride=RECORD_PITCH), :])
                lo_b, hi_b = _unpack_bf16_pairs(
                    cur[pl.ds(first + tmc * RECORD_PITCH + s, tmc, stride=RECORD_PITCH), :])
                yacc[s] += wa * lo_a + wb * lo_b
                yacc[REC_ROWS + s] += wa * hi_a + wb * hi_b
                for u in range(per_s):
                    r = (kp * REC_ROWS + s) * per_s + u
                    src_row = pl.multiple_of(slot_next_ref[0, 0, r] * RECORD_PITCH, 4)
                    dst_row = pl.multiple_of(r * RECORD_PITCH, 4)
                    pltpu.make_async_copy(y_hbm.at[pl.ds(src_row, REC_ROWS), :], nxt.at[pl.ds(dst_row, REC_ROWS), :],
                                          nxt_sem).start(priority=COMBINE_GATHER_QUEUES[u % len(COMBINE_GATHER_QUEUES)])
            return carry
        lax.fori_loop(0, n_iter, body, 0)

        @pl.when(i == n_steps - 1)
        def _():
            _wait_record_gather(n_rec, y_hbm, nxt, nxt_sem)

    @pl.when(jnp.logical_not(odd))
    def _():
        accumulate(ybuf_even, sem.at[0], ybuf_odd, sem.at[1])

    @pl.when(odd)
    def _():
        accumulate(ybuf_odd, sem.at[1], ybuf_even, sem.at[0])

    for r in range(tmc // 8):
        rows = slice(r * 8, (r + 1) * 8)
        y = jnp.concatenate([yacc[c, rows, :] for c in range(n_col)], axis=1)
        xn = xres[rows, :] + gate[0] * ((y * _rms_scale(y)) * gpost[...])
        xout[rows, :] = xn
        if has_next:
            h = ((xn * _rms_scale(xn)) * gnext[...]) * (1.0 + sc[0]) + sh[0]
            for c in range(n_col):
                yacc[c, rows, :] = h[:, c * LANES:(c + 1) * LANES]
    if has_next:
        hout[...] = jnp.concatenate([yacc[c] for c in range(n_col)], axis=1).astype(hout.dtype)


def _moe_combine(y_rec, slot_tiles, wgt, hs, w_sd, xres, gate, gpost, nxt):
    m, d = xres.shape
    n_steps, _, n_rec = slot_tiles.shape
    tmc = n_rec // TOP_K
    f = hs.shape[1]
    has_next = nxt is not None
    slot_spec = lambda shift: pl.BlockSpec(
        (1, 1, n_rec), functools.partial(lambda i, shift: (jnp.minimum(i + shift, n_steps - 1), 0, 0), shift=shift),
        memory_space=pltpu.SMEM)
    vec_spec = pl.BlockSpec((1, d), lambda i: (0, 0))
    mod_spec = pl.BlockSpec((1, 1, d), lambda i: (i * tmc // SEQ, 0, 0))
    row_spec = pl.BlockSpec((tmc, d), lambda i: (i, 0))
    in_specs = [slot_spec(0), slot_spec(1), pl.BlockSpec(memory_space=pl.ANY),
                pl.BlockSpec((tmc, TOP_K), lambda i: (i, 0)), pl.BlockSpec((tmc, f), lambda i: (i, 0)),
                pl.BlockSpec((f, d), lambda i: (0, 0)), row_spec, mod_spec, vec_spec]
    args = [slot_tiles, slot_tiles, y_rec, wgt, hs, w_sd, xres, gate, gpost.reshape(1, d)]
    out_specs = [row_spec]
    out_shape = [jax.ShapeDtypeStruct((m, d), F32)]
    if has_next:
        gnext, sh, sc = nxt
        in_specs += [vec_spec, mod_spec, mod_spec]
        args += [gnext.reshape(1, d), sh, sc]
        out_specs.append(row_spec)
        out_shape.append(jax.ShapeDtypeStruct((m, d), BF16))
    outs = pl.pallas_call(
        functools.partial(_moe_combine_kernel, has_next=has_next, n_steps=n_steps),
        grid=(n_steps,),
        in_specs=in_specs,
        out_specs=out_specs,
        out_shape=out_shape,
        scratch_shapes=[pltpu.VMEM((n_rec * RECORD_PITCH, LANES), jnp.int32),
                        pltpu.VMEM((n_rec * RECORD_PITCH, LANES), jnp.int32), pltpu.SemaphoreType.DMA((2,)),
                        pltpu.VMEM((d // LANES, tmc, LANES), F32), pltpu.VMEM((TOP_K, tmc, LANES), F32)],
        compiler_params=_cparams(("arbitrary",), 48),
        name="moe_combine",
    )(*args)
    return (outs[0], outs[1]) if has_next else (outs[0], None)


def _mod_parts(mods_layer, n_rows):
    return [mods_layer[:n_rows, j * D_MODEL:(j + 1) * D_MODEL].reshape(n_rows, 1, D_MODEL) for j in range(6)]


def _moe(h2, h2_rec, x1, gate2, gpost, nxt, router_w, router_bias, layer, w_gate, w_up, w_down, sh_gate, sh_up, sh_down):
    n = h2.shape[0]
    tm, tmc = MOE_TILE, COMBINE_TOKENS
    t_max = n * TOP_K // tm + N_EXPERTS
    idx, wgt, pos, counts, hs = _router(h2, router_w, router_bias, sh_gate, sh_up)
    tiles = (counts[:, 0] + tm - 1) // tm
    tile_end = jnp.cumsum(tiles)
    n_tiles = tile_end[-1]
    first_slot = (tile_end - tiles) * tm
    slot = jnp.sum(jnp.where(idx[:, :, None] == jnp.arange(N_EXPERTS)[None, None, :], first_slot[None, None, :], 0),
                   axis=-1) + pos
    tok = jnp.broadcast_to(jnp.arange(n, dtype=jnp.int32)[None, :], slot.shape)
    tok_slot = jnp.zeros((t_max * tm,), jnp.int32).at[slot.reshape(-1)].set(
        tok.reshape(-1), unique_indices=True, mode="promise_in_bounds")
    tile_id = jnp.minimum(jnp.arange(t_max), n_tiles - 1)
    tile_expert = jnp.sum(tile_id[:, None] >= tile_end[None, :], axis=1).astype(jnp.int32)
    y_rec = _moe_experts(h2_rec, tile_expert, tok_slot.reshape(t_max, 1, tm), layer, w_gate, w_up, w_down)
    slot_tiles = slot.reshape(TOP_K, n // tmc, tmc).transpose(1, 0, 2).reshape(n // tmc, 1, TOP_K * tmc)
    return _moe_combine(y_rec, slot_tiles, wgt.T, hs, sh_down.astype(BF16), x1, gate2, gpost, nxt)


def kernel(x, c, ctx, c_ctx, ada_w, ada_b, norm_g, attn_w_in, attn_w_out, na_rpb, q_norm_g, k_norm_g, mix_w_in, mix_w_out, conv_w, sgu_w, sgu_b, sgu_norm_g, router_w, router_bias, moe_w_gate, moe_w_up, moe_w_down, shared_w_gate, shared_w_up, shared_w_down):
    nb, s, d = x.shape
    assert (s, d) == (SEQ, D_MODEL) and ctx.shape == (nb, CTX_LEN, d)
    x_lat = x.reshape(nb * s, d)
    x_ctx = ctx.reshape(nb * CTX_LEN, d)

    c_rows = jnp.concatenate([c, c_ctx[None, :], jnp.zeros((16 - nb - 1, d), F32)], axis=0)
    mods = _adaln(c_rows, ada_w, ada_b)
    lat_mods = [_mod_parts(mods[i], nb) for i in range(DEPTH)]
    ctx_mods = _mod_parts(mods[0, nb:nb + 1], 1)

    cos, sin = _rope_tables()
    bands, pats, na_idx = _na_static_tables()

    sh1, sc1 = lat_mods[0][0], lat_mods[0][1]
    h_lat = _norm_mod(x_lat, norm_g[0, 0], sh1, sc1, SEQ)

    for i in range(DEPTH):
        _, _, g1, sh2, sc2, g2 = lat_mods[i]
        if i % 2 == 0:
            e = i // 2
            w_in = attn_w_in[e]
            kv_lo, kv_hi = NA_DIM, 3 * NA_DIM
            gk_lo = 3 * NA_DIM + GQA_Q_DIM
            h_ctx = _norm_mod(x_ctx, norm_g[i, 0], ctx_mods[0], ctx_mods[1], nb * CTX_LEN)
            w_ctx = jnp.concatenate([w_in[:, kv_lo:kv_hi], w_in[:, gk_lo:]], axis=1).astype(BF16)
            p_lat = _matmul(h_lat, w_in.astype(BF16))
            p_ctx = _matmul(h_ctx, w_ctx)
            blk = 4 * HEAD_DIM
            qg = _qk_prep(p_lat, kv_hi // blk, GQA_Q_DIM // blk, q_norm_g[e], cos, sin, True)
            kg = _qk_prep(p_lat, gk_lo // blk, GQA_KV_DIM // blk, k_norm_g[e], cos, sin, True)
            kcg = _qk_prep(p_ctx, 2 * NA_DIM // blk, GQA_KV_DIM // blk, k_norm_g[e], cos, sin, False)
            bias = _na_bias_table(na_rpb[e], na_idx)
            o_na = _na_attention(p_lat, p_ctx, bias, bands, pats)
            o_gqa = _gqa_attention(qg, kg, p_lat, kcg, p_ctx,
                                   (gk_lo + GQA_KV_DIM) // HEAD_DIM, (2 * NA_DIM + GQA_KV_DIM) // HEAD_DIM)
            parts = [o_na, o_gqa]
            w_out = attn_w_out[e].astype(BF16)
        else:
            o = i // 2
            p_mix = _matmul(h_lat, mix_w_in[o].astype(BF16))
            parts = [_conv_sgu(p_mix, conv_w[o], sgu_w[o], sgu_b[o], sgu_norm_g[o])]
            wo = mix_w_out[o]
            w_out = jnp.concatenate(
                [wo[:CONV_DIM].reshape(SGU_GROUPS, SGU_CH, d), wo[CONV_DIM:].reshape(SGU_GROUPS, SGU_CH, d)],
                axis=1).reshape(CONV_DIM + SGU_DIM, d).astype(BF16)
        x1, h2, h2_rec = _matmul_post(parts, w_out, x_lat, g1, norm_g[i, 1], norm_g[i, 2], sh2, sc2)
        nxt = None
        if i + 1 < DEPTH:
            nxt = (norm_g[i + 1, 0], lat_mods[i + 1][0], lat_mods[i + 1][1])
        x_lat, h_lat = _moe(h2, h2_rec, x1, g2, norm_g[i, 3], nxt, router_w[i], router_bias[i],
                            i, moe_w_gate, moe_w_up, moe_w_down,
                            shared_w_gate[i], shared_w_up[i], shared_w_down[i])
    return x_lat.reshape(nb, s, d)
```

```python
import functools

import numpy as np
import jax
import jax.numpy as jnp
from jax import lax
from jax.experimental import pallas as pl
from jax.experimental.pallas import tpu as pltpu

F32 = jnp.float32
BF16 = jnp.bfloat16

D_MODEL = 4096
SEQ = 2048
DEPTH = 2
GRID_W = 64
GRID_ROWS = SEQ // GRID_W
CTX_LEN = 256
HEAD_DIM = 128
N_NA_HEADS = 16
N_GQA_HEADS = 16
N_GQA_KV = 4
GQA_REP = N_GQA_HEADS // N_GQA_KV
NA_KH = 8
NA_KW = 16
NA_BAND_ROWS = NA_KH + 1
NA_KEYS = NA_BAND_ROWS * GRID_W
NA_QBLK = 2 * GRID_W
NA_NBLK = GRID_ROWS // 2
ROPE_THETA = 10000.0
CONV_DIM = D_MODEL // 2
SGU_DIM = D_MODEL // 2
SGU_GROUPS = 8
SGU_CH = SGU_DIM // SGU_GROUPS
SGU_CHUNK = 128
N_EXPERTS = 64
N_EXPERT_GROUPS = 8
EXPERTS_PER_GROUP = N_EXPERTS // N_EXPERT_GROUPS
TOPK_GROUPS = 4
TOP_K = 8
D_EXPERT = D_MODEL // 16
ROUTED_SCALE = 2.5
EPS = 1e-6
NEG = -1e30
NA_DIM = N_NA_HEADS * HEAD_DIM
GQA_Q_DIM = N_GQA_HEADS * HEAD_DIM
GQA_KV_DIM = N_GQA_KV * HEAD_DIM
ATTN_SCALE = HEAD_DIM ** -0.5
LANES = 128
REC_ROWS = D_MODEL // 2 // LANES
Y_RECORD_PITCH = REC_ROWS
MOE_TILE = 256
RECORD_PITCH = 20
COMBINE_TOKENS = 64
EXPERT_RING = 3
EXPERT_GATHER_QUEUES = (1,)
COMBINE_GATHER_QUEUES = (0, 1)

V7X_VMEM_BYTES = 64 * 1024 * 1024
MIB = 1024 * 1024


def _cparams(semantics, vmem_mib):
    assert vmem_mib * MIB < V7X_VMEM_BYTES
    return pltpu.CompilerParams(dimension_semantics=semantics, vmem_limit_bytes=vmem_mib * MIB)


def _rms_scale(x):
    return lax.rsqrt(jnp.mean(x * x, axis=-1, keepdims=True) + EPS)


def _silu(x):
    return x / (1.0 + jnp.exp(-x))


def _adaln_kernel(c_ref, w_ref, b_ref, o_ref):
    a = _silu(c_ref[...])
    o_ref[0] = jnp.dot(a.astype(BF16), w_ref[0].astype(BF16), preferred_element_type=F32) + b_ref[0]


def _adaln(c_rows, ada_w, ada_b):
    depth, d, n6 = ada_w.shape
    rows = c_rows.shape[0]
    tn = 512
    return pl.pallas_call(
        _adaln_kernel,
        grid=(depth, n6 // tn),
        in_specs=[pl.BlockSpec((rows, d), lambda l, j: (0, 0)),
                  pl.BlockSpec((1, d, tn), lambda l, j: (l, 0, j)),
                  pl.BlockSpec((1, 1, tn), lambda l, j: (l, 0, j))],
        out_specs=pl.BlockSpec((1, rows, tn), lambda l, j: (l, 0, j)),
        out_shape=jax.ShapeDtypeStruct((depth, rows, n6), F32),
        compiler_params=_cparams(("parallel", "parallel"), 40),
        name="adaln",
    )(c_rows, ada_w, ada_b.reshape(depth, 1, n6))


def _norm_mod_kernel(x_ref, g_ref, sh_ref, sc_ref, o_ref):
    x = x_ref[...]
    y = (x * _rms_scale(x)) * g_ref[...]
    o_ref[...] = (y * (1.0 + sc_ref[0]) + sh_ref[0]).astype(o_ref.dtype)


def _norm_mod(x, g, sh, sc, rows_per_mod):
    m, d = x.shape
    tm = 256
    mod_spec = pl.BlockSpec((1, 1, d), lambda i: (i * tm // rows_per_mod, 0, 0))
    return pl.pallas_call(
        _norm_mod_kernel,
        grid=(m // tm,),
        in_specs=[pl.BlockSpec((tm, d), lambda i: (i, 0)),
                  pl.BlockSpec((1, d), lambda i: (0, 0)),
                  mod_spec, mod_spec],
        out_specs=pl.BlockSpec((tm, d), lambda i: (i, 0)),
        out_shape=jax.ShapeDtypeStruct((m, d), BF16),
        compiler_params=_cparams(("parallel",), 40),
        name="norm_mod",
    )(x, g.reshape(1, d), sh, sc)


def _mm_kernel(x_ref, w_ref, o_ref):
    o_ref[...] = jnp.dot(x_ref[...], w_ref[...], preferred_element_type=F32).astype(o_ref.dtype)


def _matmul(x, w, tm=1024, tn=512):
    m, k = x.shape
    n = w.shape[1]
    tm = min(tm, m)
    return pl.pallas_call(
        _mm_kernel,
        grid=(m // tm, n // tn),
        in_specs=[pl.BlockSpec((tm, k), lambda i, j: (i, 0)),
                  pl.BlockSpec((k, tn), lambda i, j: (0, j))],
        out_specs=pl.BlockSpec((tm, tn), lambda i, j: (i, j)),
        out_shape=jax.ShapeDtypeStruct((m, n), BF16),
        compiler_params=_cparams(("parallel", "arbitrary"), 48),
        name="matmul",
    )(x, w)


def _pack_bf16_pairs(lo, hi):
    return pltpu.pack_elementwise([lo, hi], packed_dtype=BF16)


def _unpack_bf16_pairs(words):
    lo = pltpu.unpack_elementwise(words, index=0, packed_dtype=BF16, unpacked_dtype=F32)
    hi = pltpu.unpack_elementwise(words, index=1, packed_dtype=BF16, unpacked_dtype=F32)
    return lo, hi


def _store_records(ref, first_row, words, pitch):
    n = words.shape[0]
    for s in range(REC_ROWS):
        ref[pl.ds(first_row + s, n, stride=pitch), :] = words[:, s * LANES:(s + 1) * LANES]
    for s in range(REC_ROWS, pitch):
        ref[pl.ds(first_row + s, n, stride=pitch), :] = jnp.zeros((n, LANES), words.dtype)


def _load_records(ref, first_row, n, pitch):
    return jnp.concatenate([ref[pl.ds(first_row + s, n, stride=pitch), :] for s in range(REC_ROWS)], axis=1)


def _mm_post_kernel(*refs, n_parts, nk, n_chunks, row_chunk):
    lhs = refs[:n_parts]
    ws = refs[n_parts:2 * n_parts]
    xres, gate, gpost, gnext, sh, sc, xout, hout, hrec, vecs = refs[2 * n_parts:]
    k = pl.program_id(1)
    tm, d = xout.shape
    tn = d // n_chunks

    def part(c):
        acc = None
        for l, w in zip(lhs, ws):
            t = jnp.dot(l[...], w[:, c * tn:(c + 1) * tn], preferred_element_type=F32)
            acc = t if acc is None else acc + t
        return acc

    @pl.when(k == 0)
    def _():
        for c in range(n_chunks):
            xout[:, c * tn:(c + 1) * tn] = part(c)

    @pl.when(k > 0)
    def _():
        for c in range(n_chunks):
            xout[:, c * tn:(c + 1) * tn] += part(c)

    @pl.when(k == nk - 1)
    def _():
        vecs[0:1, :] = gate[0] * gpost[...]
        vecs[1:2, :] = gnext[...] * (1.0 + sc[0])

        def body(r, carry):
            rows = pl.ds(pl.multiple_of(r * row_chunk, row_chunk), row_chunk)
            y = xout[rows, :]
            xn = xres[rows, :] + (y * _rms_scale(y)) * vecs[0:1, :]
            xout[rows, :] = xn
            h = (xn * _rms_scale(xn)) * vecs[1:2, :] + sh[0]
            hout[rows, :] = h.astype(hout.dtype)
            half = d // 2
            first = pl.multiple_of(r * (row_chunk * RECORD_PITCH), row_chunk * RECORD_PITCH)
            _store_records(hrec, first, _pack_bf16_pairs(h[:, :half], h[:, half:]), RECORD_PITCH)
            return carry
        lax.fori_loop(0, tm // row_chunk, body, 0)


def _matmul_post(lhs_parts, w, xres, gate, gpost, gnext, sh, sc, tm=512):
    m, d = xres.shape
    n_parts = len(lhs_parts)
    kp = lhs_parts[0].shape[1]
    assert all(l.shape == (m, kp) for l in lhs_parts) and w.shape == (n_parts * kp, d)
    tk = 512 // n_parts
    nk = kp // tk
    vec_spec = pl.BlockSpec((1, d), lambda i, k: (0, 0))
    mod_spec = pl.BlockSpec((1, 1, d), lambda i, k: (i * tm // SEQ, 0, 0))
    row_spec = pl.BlockSpec((tm, d), lambda i, k: (i, 0))
    in_specs = [pl.BlockSpec((tm, tk), lambda i, k: (i, k)) for _ in range(n_parts)]
    in_specs += [pl.BlockSpec((tk, d), functools.partial(lambda i, k, off: (k + off, 0), off=p * nk))
                 for p in range(n_parts)]
    in_specs += [pl.BlockSpec((tm, d), lambda i, k: (i, 0), pipeline_mode=pl.Buffered(1)),
                 mod_spec, vec_spec, vec_spec, mod_spec, mod_spec]
    args = list(lhs_parts) + [w] * n_parts + [xres, gate, gpost.reshape(1, d), gnext.reshape(1, d), sh, sc]
    return pl.pallas_call(
        functools.partial(_mm_post_kernel, n_parts=n_parts, nk=nk, n_chunks=4, row_chunk=32),
        grid=(m // tm, nk),
        in_specs=in_specs,
        out_specs=[row_spec, row_spec, pl.BlockSpec((tm * RECORD_PITCH, LANES), lambda i, k: (i, 0))],
        out_shape=[jax.ShapeDtypeStruct((m, d), F32), jax.ShapeDtypeStruct((m, d), BF16),
                   jax.ShapeDtypeStruct((m * RECORD_PITCH, LANES), jnp.int32)],
        scratch_shapes=[pltpu.VMEM((8, d), F32)],
        compiler_params=_cparams(("parallel", "arbitrary"), 56),
        name="matmul_post",
    )(*args)


def _qk_prep_kernel(x_ref, g_ref, cos_ref, sin_ref, o_ref, *, rope):
    n_heads = x_ref.shape[1] // HEAD_DIM
    for hh in range(n_heads):
        cols = slice(hh * HEAD_DIM, (hh + 1) * HEAD_DIM)
        x = x_ref[:, cols].astype(F32)
        y = (x * _rms_scale(x)) * g_ref[...]
        if rope:
            lane = lax.broadcasted_iota(jnp.int32, y.shape, 1)
            quarter = HEAD_DIM // 4
            partner = jnp.where(jnp.bitwise_and(lane, 2 * quarter - 1) < quarter,
                                pltpu.roll(y, HEAD_DIM - quarter, axis=1), pltpu.roll(y, quarter, axis=1))
            y = y * cos_ref[...] + partner * sin_ref[...]
        o_ref[:, cols] = y.astype(o_ref.dtype)


def _qk_prep(p, col_block0, n_col_blocks, g, cos, sin, rope):
    m = p.shape[0]
    tm, tc = 512, 4 * HEAD_DIM
    pos_blocks = SEQ // tm
    return pl.pallas_call(
        functools.partial(_qk_prep_kernel, rope=rope),
        grid=(m // tm, n_col_blocks),
        in_specs=[pl.BlockSpec((tm, tc), lambda i, j: (i, col_block0 + j)),
                  pl.BlockSpec((1, HEAD_DIM), lambda i, j: (0, 0)),
                  pl.BlockSpec((tm, HEAD_DIM), lambda i, j: (i % pos_blocks, 0)),
                  pl.BlockSpec((tm, HEAD_DIM), lambda i, j: (i % pos_blocks, 0))],
        out_specs=pl.BlockSpec((tm, tc), lambda i, j: (i, j)),
        out_shape=jax.ShapeDtypeStruct((m, tc * n_col_blocks), BF16),
        compiler_params=_cparams(("parallel", "parallel"), 32),
        name="qk_prep",
    )(p, g.reshape(1, HEAD_DIM), cos, sin)


def _rope_tables():
    half = HEAD_DIM // 4
    t = jnp.arange(SEQ)
    freqs = ROPE_THETA ** (-jnp.arange(half, dtype=F32) / half)
    ang_r = (t // GRID_W).astype(F32)[:, None] * freqs[None, :]
    ang_c = (t % GRID_W).astype(F32)[:, None] * freqs[None, :]
    cos = jnp.concatenate([jnp.cos(ang_r)] * 2 + [jnp.cos(ang_c)] * 2, axis=-1)
    sin = jnp.concatenate([-jnp.sin(ang_r), jnp.sin(ang_r), -jnp.sin(ang_c), jnp.sin(ang_c)], axis=-1)
    return cos, sin


def _na_static_tables():
    rows, kh, kw = GRID_ROWS, min(NA_KH, GRID_ROWS), NA_KW
    nb_rows = min(kh + 1, rows)
    assert nb_rows == NA_BAND_ROWS
    col = np.arange(GRID_W)
    col_start = np.clip(col - kw // 2, 0, GRID_W - kw)
    patterns, pat_of_block, bands = [], [], []
    for b in range(NA_NBLK):
        q_rows = 2 * b + np.arange(2)
        row_start = np.clip(q_rows - kh // 2, 0, rows - kh)
        band = int(np.clip(row_start[0], 0, rows - nb_rows))
        key = (tuple(q_rows - band), tuple(row_start - band))
        if key not in patterns:
            patterns.append(key)
        pat_of_block.append(patterns.index(key))
        bands.append(band)
    idx = np.full((len(patterns), NA_QBLK, NA_KEYS), -1, np.int32)
    for p, (q_rel, rs_rel) in enumerate(patterns):
        qr = np.repeat(np.array(q_rel), GRID_W)
        qc = np.tile(col, 2)
        rs = np.repeat(np.array(rs_rel), GRID_W)
        cs = np.tile(col_start, 2)
        kr = np.repeat(np.arange(nb_rows), GRID_W)
        kc = np.tile(col, nb_rows)
        in_win = ((kr[None, :] >= rs[:, None]) & (kr[None, :] < rs[:, None] + kh)
                  & (kc[None, :] >= cs[:, None]) & (kc[None, :] < cs[:, None] + kw))
        dr = np.clip(kr[None, :] - qr[:, None] + NA_KH - 1, 0, 2 * NA_KH - 2)
        dc = np.clip(kc[None, :] - qc[:, None] + NA_KW - 1, 0, 2 * NA_KW - 2)
        idx[p] = np.where(in_win, dr * (2 * NA_KW - 1) + dc, -1)
    return np.array(bands, np.int32), np.array(pat_of_block, np.int32), idx


def _na_bias_kernel(idx_ref, rpb_ref, o_ref):
    idx = idx_ref[0]
    n_tab = rpb_ref.shape[1]
    onehot = (lax.broadcasted_iota(jnp.int32, (n_tab, idx.shape[1]), 0) == idx).astype(BF16)
    r = rpb_ref[...]
    hi = r.astype(BF16)
    r1 = r - hi.astype(F32)
    mid = r1.astype(BF16)
    lo = (r1 - mid.astype(F32)).astype(BF16)
    val = (jnp.dot(hi, onehot, preferred_element_type=F32) + jnp.dot(mid, onehot, preferred_element_type=F32)
           + jnp.dot(lo, onehot, preferred_element_type=F32))
    o_ref[0] = jnp.where(idx < 0, NEG, val)


def _na_bias_table(rpb, idx):
    n_pat = idx.shape[0]
    h = rpb.shape[0]
    n_tab = 512
    flat = NA_QBLK * NA_KEYS
    tile = 8 * NA_KEYS
    rpb_flat = jnp.pad(rpb.reshape(h, -1), ((0, 0), (0, n_tab - rpb.shape[1] * rpb.shape[2])))
    out = pl.pallas_call(
        _na_bias_kernel,
        grid=(n_pat, flat // tile),
        in_specs=[pl.BlockSpec((1, 1, tile), lambda p, j: (p, 0, j)),
                  pl.BlockSpec((h, n_tab), lambda p, j: (0, 0))],
        out_specs=pl.BlockSpec((1, h, tile), lambda p, j: (p, 0, j)),
        out_shape=jax.ShapeDtypeStruct((n_pat, h, flat), F32),
        compiler_params=_cparams(("parallel", "parallel"), 32),
        name="na_bias",
    )(jnp.asarray(idx.reshape(n_pat, 1, flat)), rpb_flat)
    return out.reshape(n_pat, h, NA_QBLK, NA_KEYS)


_NT = (((1,), (1,)), ((), ()))


def _na_attn_kernel(q_ref, k_ref, v_ref, kc_ref, vc_ref, bias_ref, o_ref, *, bands, pats):
    kc = kc_ref[...]
    vc = vc_ref[...]

    for n in range(NA_NBLK):
        q_rows = slice(n * NA_QBLK, (n + 1) * NA_QBLK)
        key_rows = slice(bands[n] * GRID_W, bands[n] * GRID_W + NA_KEYS)
        q = q_ref[q_rows, :]
        bias = bias_ref[pats[n], 0]
        s = lax.dot_general(q, k_ref[key_rows, :], _NT, preferred_element_type=F32) * ATTN_SCALE + bias
        s = jnp.where(bias <= 0.5 * NEG, NEG, s)
        s_ctx = lax.dot_general(q, kc, _NT, preferred_element_type=F32) * ATTN_SCALE
        m = jnp.maximum(jnp.max(s, axis=-1, keepdims=True), jnp.max(s_ctx, axis=-1, keepdims=True))
        p = jnp.exp(s - m)
        p_ctx = jnp.exp(s_ctx - m)
        denom = jnp.sum(p, axis=-1, keepdims=True) + jnp.sum(p_ctx, axis=-1, keepdims=True)
        o = (jnp.dot(p.astype(BF16), v_ref[key_rows, :], preferred_element_type=F32)
             + jnp.dot(p_ctx.astype(BF16), vc, preferred_element_type=F32))
        o_ref[q_rows, :] = (o / denom).astype(o_ref.dtype)


def _na_attention(p_lat, p_ctx, bias, bands, pats):
    n_tok = p_lat.shape[0]
    nb = n_tok // SEQ
    n_pat = bias.shape[0]
    hd = HEAD_DIM
    return pl.pallas_call(
        functools.partial(_na_attn_kernel, bands=tuple(int(b) for b in bands), pats=tuple(int(p) for p in pats)),
        grid=(N_NA_HEADS, nb),
        in_specs=[pl.BlockSpec((SEQ, hd), lambda h, b: (b, h)),
                  pl.BlockSpec((SEQ, hd), lambda h, b: (b, N_NA_HEADS + h)),
                  pl.BlockSpec((SEQ, hd), lambda h, b: (b, 2 * N_NA_HEADS + h)),
                  pl.BlockSpec((CTX_LEN, hd), lambda h, b: (b, h)),
                  pl.BlockSpec((CTX_LEN, hd), lambda h, b: (b, N_NA_HEADS + h)),
                  pl.BlockSpec((n_pat, 1, NA_QBLK, NA_KEYS), lambda h, b: (0, h, 0, 0))],
        out_specs=pl.BlockSpec((SEQ, hd), lambda h, b: (b, h)),
        out_shape=jax.ShapeDtypeStruct((n_tok, NA_DIM), BF16),
        compiler_params=_cparams(("parallel", "parallel"), 32),
        name="na_attention",
    )(p_lat, p_lat, p_lat, p_ctx, p_ctx, bias)


def _gqa_attn_kernel(q_ref, k_ref, v_ref, kc_ref, vc_ref, o_ref):
    k = k_ref[...]
    v = v_ref[...]
    kc = kc_ref[...]
    vc = vc_ref[...]
    for r in range(GQA_REP):
        cols = slice(r * HEAD_DIM, (r + 1) * HEAD_DIM)
        q = q_ref[:, cols]
        s = lax.dot_general(q, k, _NT, preferred_element_type=F32) * ATTN_SCALE
        s_ctx = lax.dot_general(q, kc, _NT, preferred_element_type=F32) * ATTN_SCALE
        m = jnp.maximum(jnp.max(s, axis=-1, keepdims=True), jnp.max(s_ctx, axis=-1, keepdims=True))
        p = jnp.exp(s - m)
        p_ctx = jnp.exp(s_ctx - m)
        denom = jnp.sum(p, axis=-1, keepdims=True) + jnp.sum(p_ctx, axis=-1, keepdims=True)
        o = (jnp.dot(p.astype(BF16), v, preferred_element_type=F32)
             + jnp.dot(p_ctx.astype(BF16), vc, preferred_element_type=F32))
        o_ref[:, cols] = (o / denom).astype(o_ref.dtype)


def _gqa_attention(qg, kg, p_lat, kcg, p_ctx, v_col_block, vc_col_block, tq=256):
    n_tok = qg.shape[0]
    nb = n_tok // SEQ
    hd = HEAD_DIM
    qt = SEQ // tq
    return pl.pallas_call(
        _gqa_attn_kernel,
        grid=(nb, N_GQA_KV, qt),
        in_specs=[pl.BlockSpec((tq, GQA_REP * hd), lambda b, g, t: (b * qt + t, g)),
                  pl.BlockSpec((SEQ, hd), lambda b, g, t: (b, g)),
                  pl.BlockSpec((SEQ, hd), lambda b, g, t: (b, v_col_block + g)),
                  pl.BlockSpec((CTX_LEN, hd), lambda b, g, t: (b, g)),
                  pl.BlockSpec((CTX_LEN, hd), lambda b, g, t: (b, vc_col_block + g))],
        out_specs=pl.BlockSpec((tq, GQA_REP * hd), lambda b, g, t: (b * qt + t, g)),
        out_shape=jax.ShapeDtypeStruct((n_tok, GQA_Q_DIM), BF16),
        compiler_params=_cparams(("parallel", "parallel", "parallel"), 48),
        name="gqa_attention",
    )(qg, kg, p_lat, kcg, p_ctx)


def _mixer_kernel(gb_ref, gc_ref, xc_ref, u_ref, v_ref, cw_ref, sw_ref, sb_ref, sg_ref, o_ref):
    s, c = gb_ref.shape
    z = gc_ref[...].astype(F32) * xc_ref[...].astype(F32)
    row = lax.broadcasted_iota(jnp.int32, (s, c), 0)
    z_prev = jnp.where(row == 0, 0.0, pltpu.roll(z, 1, axis=0))
    z_next = jnp.where(row == s - 1, 0.0, pltpu.roll(z, s - 1, axis=0))
    cw = cw_ref[...]
    conv = z_prev * cw[0:1, :] + z * cw[1:2, :] + z_next * cw[2:3, :]
    o_ref[:, :c] = (gb_ref[...].astype(F32) * conv).astype(o_ref.dtype)
    w = sw_ref[0].astype(BF16)
    for n in range(s // SGU_CHUNK):
        rows = slice(n * SGU_CHUNK, (n + 1) * SGU_CHUNK)
        vv = v_ref[rows, :].astype(F32)
        vg = (vv * _rms_scale(vv)) * sg_ref[0]
        mix = jnp.dot(w, vg.astype(BF16), preferred_element_type=F32) + sb_ref[0]
        o_ref[rows, c:] = (u_ref[rows, :].astype(F32) * mix).astype(o_ref.dtype)


def _conv_sgu(p, conv_w, sgu_w, sgu_b, sgu_g):
    n_tok = p.shape[0]
    nb = n_tok // SEQ
    c = SGU_CH
    blocks = CONV_DIM // c
    col = lambda off: pl.BlockSpec((SEQ, c), functools.partial(lambda b, g, off: (b, off + g), off=off))
    return pl.pallas_call(
        _mixer_kernel,
        grid=(nb, SGU_GROUPS),
        in_specs=[col(0), col(blocks), col(2 * blocks), col(3 * blocks), col(4 * blocks),
                  pl.BlockSpec((3, c), lambda b, g: (0, g)),
                  pl.BlockSpec((1, SGU_CHUNK, SGU_CHUNK), lambda b, g: (g, 0, 0)),
                  pl.BlockSpec((1, SGU_CHUNK, 1), lambda b, g: (g, 0, 0)),
                  pl.BlockSpec((1, 1, c), lambda b, g: (g, 0, 0))],
        out_specs=pl.BlockSpec((SEQ, 2 * c), lambda b, g: (b, g)),
        out_shape=jax.ShapeDtypeStruct((n_tok, CONV_DIM + SGU_DIM), BF16),
        compiler_params=_cparams(("parallel", "parallel"), 48),
        name="conv_sgu",
    )(p, p, p, p, p, conv_w, sgu_w, sgu_b.reshape(SGU_GROUPS, SGU_CHUNK, 1), sgu_g.reshape(SGU_GROUPS, 1, c))


def _router_kernel(x_ref, whi_ref, wlo_ref, bias_ref, tri_ref, wsh_ref,
                   idx_ref, wgt_ref, pos_ref, cnt_ref, hs_ref, carry_ref):
    @pl.when(pl.program_id(0) == 0)
    def _():
        carry_ref[...] = jnp.zeros_like(carry_ref)

    x = x_ref[...]
    hsu = jnp.dot(x, wsh_ref[...], preferred_element_type=F32)
    f = hsu.shape[1] // 2
    hs_ref[...] = (_silu(hsu[:, :f]) * hsu[:, f:]).astype(hs_ref.dtype)

    logits = (lax.dot_general(whi_ref[...], x, _NT, preferred_element_type=F32)
              + lax.dot_general(wlo_ref[...], x, _NT, preferred_element_type=F32))
    scores = 1.0 / (1.0 + jnp.exp(-logits))
    sel = scores + bias_ref[...]
    n_e, tm = sel.shape
    epg = EXPERTS_PER_GROUP
    groups = [sel[g * epg:(g + 1) * epg, :] for g in range(N_EXPERT_GROUPS)]
    gscore = []
    for sg in groups:
        top1 = jnp.max(sg, axis=0, keepdims=True)
        is_top = sg == top1
        n_top = jnp.sum(is_top.astype(F32), axis=0, keepdims=True)
        second = jnp.max(jnp.where(is_top, -jnp.inf, sg), axis=0, keepdims=True)
        gscore.append(top1 + jnp.where(n_top >= 2.0, top1, second))
    masked = []
    for g in range(N_EXPERT_GROUPS):
        rank = jnp.zeros((1, tm), F32)
        for o in range(N_EXPERT_GROUPS):
            if o == g:
                continue
            beats = (gscore[o] >= gscore[g]) if o < g else (gscore[o] > gscore[g])
            rank = rank + beats.astype(F32)
        masked.append(jnp.where(rank < float(TOPK_GROUPS), groups[g], -jnp.inf))
    v = jnp.concatenate(masked, axis=0)
    row = lax.broadcasted_iota(jnp.int32, (n_e, tm), 0)
    chosen = jnp.zeros((n_e, tm), F32)
    picks, top_s = [], []
    for _ in range(TOP_K):
        best = jnp.max(v, axis=0, keepdims=True)
        first = jnp.min(jnp.where(v == best, row, n_e), axis=0, keepdims=True)
        pick = row == first
        picks.append(first)
        top_s.append(jnp.sum(jnp.where(pick, scores, 0.0), axis=0, keepdims=True))
        chosen = jnp.where(pick, 1.0, chosen)
        v = jnp.where(pick, -jnp.inf, v)
    total = top_s[0]
    for s in top_s[1:]:
        total = total + s
    running = jnp.dot(chosen.astype(BF16), tri_ref[...], preferred_element_type=F32) + carry_ref[...]
    pos = [jnp.sum(jnp.where(row == first, running - 1.0, 0.0), axis=0, keepdims=True) for first in picks]
    idx_ref[...] = jnp.concatenate(picks, axis=0)
    wgt_ref[...] = jnp.concatenate([s / total * ROUTED_SCALE for s in top_s], axis=0)
    pos_ref[...] = jnp.concatenate(pos, axis=0).astype(jnp.int32)
    carry_ref[...] += jnp.sum(chosen, axis=1, keepdims=True)
    cnt_ref[...] = carry_ref[...].astype(jnp.int32)


def _router(h, router_w, router_bias, sh_gate, sh_up, tm=1024):
    n, d = h.shape
    f = sh_gate.shape[1]
    wt = router_w.T
    w_hi = wt.astype(BF16)
    w_lo = (wt - w_hi.astype(F32)).astype(BF16)
    tri = (jnp.arange(tm)[:, None] <= jnp.arange(tm)[None, :]).astype(BF16)
    w_sh = jnp.concatenate([sh_gate, sh_up], axis=1).astype(BF16)
    fixed = lambda shape: pl.BlockSpec(shape, lambda i: (0, 0))
    per_tok = pl.BlockSpec((TOP_K, tm), lambda i: (0, i))
    return pl.pallas_call(
        _router_kernel,
        grid=(n // tm,),
        in_specs=[pl.BlockSpec((tm, d), lambda i: (i, 0)), fixed((N_EXPERTS, d)), fixed((N_EXPERTS, d)),
                  fixed((N_EXPERTS, 1)), fixed((tm, tm)), fixed((d, 2 * f))],
        out_specs=[per_tok, per_tok, per_tok, fixed((N_EXPERTS, 1)), pl.BlockSpec((tm, f), lambda i: (i, 0))],
        out_shape=[jax.ShapeDtypeStruct((TOP_K, n), jnp.int32), jax.ShapeDtypeStruct((TOP_K, n), F32),
                   jax.ShapeDtypeStruct((TOP_K, n), jnp.int32), jax.ShapeDtypeStruct((N_EXPERTS, 1), jnp.int32),
                   jax.ShapeDtypeStruct((n, f), BF16)],
        scratch_shapes=[pltpu.VMEM((N_EXPERTS, 1), F32)],
        compiler_params=_cparams(("arbitrary",), 48),
        name="router",
    )(h, w_hi, w_lo, router_bias.reshape(N_EXPERTS, 1), tri, w_sh)


def _start_record_gather(idx_ref, n, src_hbm, dst, sem, pitch, priorities, unroll=8, src_pitch=None):
    def body(j, carry):
        for u in range(unroll):
            r = j * unroll + u
            src_row = pl.multiple_of(idx_ref[0, 0, r] * (src_pitch or pitch), 4)
            dst_row = pl.multiple_of(r * pitch, 4)
            pltpu.make_async_copy(src_hbm.at[pl.ds(src_row, REC_ROWS), :], dst.at[pl.ds(dst_row, REC_ROWS), :],
                                  sem).start(priority=priorities[u % len(priorities)])
        return carry
    lax.fori_loop(0, n // unroll, body, 0)


def _wait_record_gather(n, src_hbm, dst, sem):
    pltpu.make_async_copy(src_hbm.at[pl.ds(0, n * REC_ROWS), :], dst.at[pl.ds(0, n * REC_ROWS), :], sem).wait()


def _issue_records(idx_ref, r0, r1, src_hbm, dst, sem, pitch, priorities):
    for r in range(r0, r1):
        src_row = pl.multiple_of(idx_ref[0, 0, r] * pitch, 4)
        pltpu.make_async_copy(src_hbm.at[pl.ds(src_row, REC_ROWS), :], dst.at[pl.ds(r * pitch, REC_ROWS), :],
                              sem).start(priority=priorities[r % len(priorities)])


def _moe_expert_kernel(te_ref, tok_ref, tok_next_ref, tok_ahead_ref, x_hbm, wg_ref, wu_ref, wd_ref, y_ref,
                       xbuf, sem, wg_bf, wu_bf, wd_bf):
    i = pl.program_id(0)
    last = pl.num_programs(0) - 1
    slot = lax.rem(i, EXPERT_RING)
    slot_next = lax.rem(i + 1, EXPERT_RING)
    slot_ahead = lax.rem(i + 2, EXPERT_RING)
    tm = tok_ref.shape[2]
    half = D_MODEL // 2
    chunk = 4 * LANES
    n_chunks = half // chunk

    @pl.when(i == 0)
    def _():
        _start_record_gather(tok_ref, tm, x_hbm, xbuf.at[0], sem.at[0], RECORD_PITCH, EXPERT_GATHER_QUEUES)
        _start_record_gather(tok_next_ref, tm, x_hbm, xbuf.at[1], sem.at[1], RECORD_PITCH, EXPERT_GATHER_QUEUES)

    @pl.when(jnp.logical_or(i == 0, te_ref[i] != te_ref[jnp.maximum(i - 1, 0)]))
    def _():
        wg_bf[...] = wg_ref[0, 0].astype(BF16)
        wu_bf[...] = wu_ref[0, 0].astype(BF16)
        wd_bf[...] = wd_ref[0, 0].astype(BF16)

    _wait_record_gather(tm, x_hbm, xbuf.at[slot], sem.at[slot])
    lo, hi = _unpack_bf16_pairs(_load_records(xbuf.at[slot], 0, tm, RECORD_PITCH))
    lo = lo.astype(BF16)
    hi = hi.astype(BF16)
    n_groups = n_chunks + 2
    bounds = [tm * g // n_groups for g in range(n_groups + 1)]
    prefetch = lambda g: _issue_records(tok_ahead_ref, bounds[g], bounds[g + 1], x_hbm, xbuf.at[slot_ahead],
                                        sem.at[slot_ahead], RECORD_PITCH, EXPERT_GATHER_QUEUES)
    prefetch(0)
    pre_g = (jnp.dot(lo, wg_bf[:half, :], preferred_element_type=F32)
             + jnp.dot(hi, wg_bf[half:, :], preferred_element_type=F32))
    prefetch(1)
    pre_u = (jnp.dot(lo, wu_bf[:half, :], preferred_element_type=F32)
             + jnp.dot(hi, wu_bf[half:, :], preferred_element_type=F32))
    hid = (_silu(pre_g) * pre_u).astype(BF16)
    for c in range(n_chunks):
        prefetch(2 + c)
        y_lo = jnp.dot(hid, wd_bf[:, c * chunk:(c + 1) * chunk], preferred_element_type=F32)
        y_hi = jnp.dot(hid, wd_bf[:, half + c * chunk:half + (c + 1) * chunk], preferred_element_type=F32)
        words = _pack_bf16_pairs(y_lo, y_hi)
        for s in range(chunk // LANES):
            y_ref[pl.ds(c * (chunk // LANES) + s, tm, stride=Y_RECORD_PITCH), :] = words[:, s * LANES:(s + 1) * LANES]
    for s in range(REC_ROWS, Y_RECORD_PITCH):
        y_ref[pl.ds(s, tm, stride=Y_RECORD_PITCH), :] = jnp.zeros((tm, LANES), y_ref.dtype)

    @pl.when(i == last)
    def _():
        _wait_record_gather(tm, x_hbm, xbuf.at[slot_next], sem.at[slot_next])
        _wait_record_gather(tm, x_hbm, xbuf.at[slot_ahead], sem.at[slot_ahead])


def _moe_experts(x_rec, tile_expert, tok_slot, layer, w_gate, w_up, w_down):
    t_max, _, tm = tok_slot.shape
    _, e, d, f = w_gate.shape
    tok_spec = lambda shift: pl.BlockSpec(
        (1, 1, tm), functools.partial(lambda i, te, shift: (jnp.minimum(i + shift, t_max - 1), 0, 0), shift=shift),
        memory_space=pltpu.SMEM)
    grid_spec = pltpu.PrefetchScalarGridSpec(
        num_scalar_prefetch=1,
        grid=(t_max,),
        in_specs=[tok_spec(0), tok_spec(1), tok_spec(2),
                  pl.BlockSpec(memory_space=pl.ANY),
                  pl.BlockSpec((1, 1, d, f), lambda i, te: (layer, te[i], 0, 0)),
                  pl.BlockSpec((1, 1, d, f), lambda i, te: (layer, te[i], 0, 0)),
                  pl.BlockSpec((1, 1, f, d), lambda i, te: (layer, te[i], 0, 0))],
        out_specs=pl.BlockSpec((tm * Y_RECORD_PITCH, LANES), lambda i, te: (i, 0)),
        scratch_shapes=[pltpu.VMEM((EXPERT_RING, tm * RECORD_PITCH, LANES), jnp.int32),
                        pltpu.SemaphoreType.DMA((EXPERT_RING,)),
                        pltpu.VMEM((d, f), BF16), pltpu.VMEM((d, f), BF16), pltpu.VMEM((f, d), BF16)],
    )
    return pl.pallas_call(
        _moe_expert_kernel,
        grid_spec=grid_spec,
        out_shape=jax.ShapeDtypeStruct((t_max * tm * Y_RECORD_PITCH, LANES), jnp.int32),
        compiler_params=_cparams(("arbitrary",), 56),
        name="moe_experts",
    )(tile_expert, tok_slot, tok_slot, tok_slot, x_rec, w_gate, w_up, w_down)


def _moe_combine_kernel(*refs, has_next, n_steps):
    if has_next:
        (slot_ref, slot_next_ref, y_hbm, wgt_ref, hs_ref, wsd_ref, xres, gate, gpost, gnext, sh, sc,
         xout, hout, ybuf_even, ybuf_odd, sem, yacc, wbuf) = refs
    else:
        (slot_ref, slot_next_ref, y_hbm, wgt_ref, hs_ref, wsd_ref, xres, gate, gpost,
         xout, ybuf_even, ybuf_odd, sem, yacc, wbuf) = refs
    i = pl.program_id(0)
    odd = lax.rem(i, 2) == 1
    tmc = xout.shape[0]
    n_rec = TOP_K * tmc
    n_col = xout.shape[1] // LANES

    @pl.when(i == 0)
    def _():
        _start_record_gather(slot_ref, n_rec, y_hbm, ybuf_even, sem.at[0], RECORD_PITCH, COMBINE_GATHER_QUEUES,
                             src_pitch=Y_RECORD_PITCH)

    hs = hs_ref[...]
    for c in range(n_col):
        yacc[c] = jnp.dot(hs, wsd_ref[:, c * LANES:(c + 1) * LANES], preferred_element_type=F32)
    wgt = wgt_ref[...]
    for k in range(TOP_K):
        wbuf[k] = jnp.broadcast_to(wgt[:, k:k + 1], (tmc, LANES))
    n_iter = TOP_K // 2
    per_s = n_rec // (n_iter * REC_ROWS)

    def accumulate(cur, cur_sem, nxt, nxt_sem):
        _wait_record_gather(n_rec, y_hbm, cur, cur_sem)

        def body(kp, carry):
            ka = 2 * kp
            wa = wbuf[ka]
            wb = wbuf[ka + 1]
            first = ka * (tmc * RECORD_PITCH)
            for s in range(REC_ROWS):
                lo_a, hi_a = _unpack_bf16_pairs(cur[pl.ds(first + s, tmc, stride=RECORD_PITCH), :])
                lo_b, hi_b = _unpack_bf16_pairs(
                    cur[pl.ds(first + tmc * RECORD_PITCH + s, tmc, stride=RECORD_PITCH), :])
                yacc[s] += wa * lo_a + wb * lo_b
                yacc[REC_ROWS + s] += wa * hi_a + wb * hi_b
                for u in range(per_s):
                    r = (kp * REC_ROWS + s) * per_s + u
                    src_row = pl.multiple_of(slot_next_ref[0, 0, r] * Y_RECORD_PITCH, 4)
                    dst_row = pl.multiple_of(r * RECORD_PITCH, 4)
                    pltpu.make_async_copy(y_hbm.at[pl.ds(src_row, REC_ROWS), :], nxt.at[pl.ds(dst_row, REC_ROWS), :],
                                          nxt_sem).start(priority=COMBINE_GATHER_QUEUES[u % len(COMBINE_GATHER_QUEUES)])
            return carry
        lax.fori_loop(0, n_iter, body, 0)

        @pl.when(i == n_steps - 1)
        def _():
            _wait_record_gather(n_rec, y_hbm, nxt, nxt_sem)

    @pl.when(jnp.logical_not(odd))
    def _():
        accumulate(ybuf_even, sem.at[0], ybuf_odd, sem.at[1])

    @pl.when(odd)
    def _():
        accumulate(ybuf_odd, sem.at[1], ybuf_even, sem.at[0])

    for r in range(tmc // 8):
        rows = slice(r * 8, (r + 1) * 8)
        y = jnp.concatenate([yacc[c, rows, :] for c in range(n_col)], axis=1)
        xn = xres[rows, :] + gate[0] * ((y * _rms_scale(y)) * gpost[...])
        xout[rows, :] = xn
        if has_next:
            h = ((xn * _rms_scale(xn)) * gnext[...]) * (1.0 + sc[0]) + sh[0]
            for c in range(n_col):
                yacc[c, rows, :] = h[:, c * LANES:(c + 1) * LANES]
    if has_next:
        hout[...] = jnp.concatenate([yacc[c] for c in range(n_col)], axis=1).astype(hout.dtype)


def _moe_combine(y_rec, slot_tiles, wgt, hs, w_sd, xres, gate, gpost, nxt):
    m, d = xres.shape
    n_steps, _, n_rec = slot_tiles.shape
    tmc = n_rec // TOP_K
    f = hs.shape[1]
    has_next = nxt is not None
    slot_spec = lambda shift: pl.BlockSpec(
        (1, 1, n_rec), functools.partial(lambda i, shift: (jnp.minimum(i + shift, n_steps - 1), 0, 0), shift=shift),
        memory_space=pltpu.SMEM)
    vec_spec = pl.BlockSpec((1, d), lambda i: (0, 0))
    mod_spec = pl.BlockSpec((1, 1, d), lambda i: (i * tmc // SEQ, 0, 0))
    row_spec = pl.BlockSpec((tmc, d), lambda i: (i, 0))
    in_specs = [slot_spec(0), slot_spec(1), pl.BlockSpec(memory_space=pl.ANY),
                pl.BlockSpec((tmc, TOP_K), lambda i: (i, 0)), pl.BlockSpec((tmc, f), lambda i: (i, 0)),
                pl.BlockSpec((f, d), lambda i: (0, 0)), row_spec, mod_spec, vec_spec]
    args = [slot_tiles, slot_tiles, y_rec, wgt, hs, w_sd, xres, gate, gpost.reshape(1, d)]
    out_specs = [row_spec]
    out_shape = [jax.ShapeDtypeStruct((m, d), F32)]
    if has_next:
        gnext, sh, sc = nxt
        in_specs += [vec_spec, mod_spec, mod_spec]
        args += [gnext.reshape(1, d), sh, sc]
        out_specs.append(row_spec)
        out_shape.append(jax.ShapeDtypeStruct((m, d), BF16))
    outs = pl.pallas_call(
        functools.partial(_moe_combine_kernel, has_next=has_next, n_steps=n_steps),
        grid=(n_steps,),
        in_specs=in_specs,
        out_specs=out_specs,
        out_shape=out_shape,
        scratch_shapes=[pltpu.VMEM((n_rec * RECORD_PITCH, LANES), jnp.int32),
                        pltpu.VMEM((n_rec * RECORD_PITCH, LANES), jnp.int32), pltpu.SemaphoreType.DMA((2,)),
                        pltpu.VMEM((d // LANES, tmc, LANES), F32), pltpu.VMEM((TOP_K, tmc, LANES), F32)],
        compiler_params=_cparams(("arbitrary",), 48),
        name="moe_combine",
    )(*args)
    return (outs[0], outs[1]) if has_next else (outs[0], None)


def _mod_parts(mods_layer, n_rows):
    return [mods_layer[:n_rows, j * D_MODEL:(j + 1) * D_MODEL].reshape(n_rows, 1, D_MODEL) for j in range(6)]


def _moe(h2, h2_rec, x1, gate2, gpost, nxt, router_w, router_bias, layer, w_gate, w_up, w_down, sh_gate, sh_up, sh_down):
    n = h2.shape[0]
    tm, tmc = MOE_TILE, COMBINE_TOKENS
    t_max = n * TOP_K // tm + N_EXPERTS
    idx, wgt, pos, counts, hs = _router(h2, router_w, router_bias, sh_gate, sh_up)
    tiles = (counts[:, 0] + tm - 1) // tm
    tile_end = jnp.cumsum(tiles)
    n_tiles = tile_end[-1]
    first_slot = (tile_end - tiles) * tm
    slot = jnp.sum(jnp.where(idx[:, :, None] == jnp.arange(N_EXPERTS)[None, None, :], first_slot[None, None, :], 0),
                   axis=-1) + pos
    tok = jnp.broadcast_to(jnp.arange(n, dtype=jnp.int32)[None, :], slot.shape)
    tok_slot = jnp.zeros((t_max * tm,), jnp.int32).at[slot.reshape(-1)].set(
        tok.reshape(-1), unique_indices=True, mode="promise_in_bounds")
    tile_id = jnp.minimum(jnp.arange(t_max), n_tiles - 1)
    tile_expert = jnp.sum(tile_id[:, None] >= tile_end[None, :], axis=1).astype(jnp.int32)
    y_rec = _moe_experts(h2_rec, tile_expert, tok_slot.reshape(t_max, 1, tm), layer, w_gate, w_up, w_down)
    slot_tiles = slot.reshape(TOP_K, n // tmc, tmc).transpose(1, 0, 2).reshape(n // tmc, 1, TOP_K * tmc)
    return _moe_combine(y_rec, slot_tiles, wgt.T, hs, sh_down.astype(BF16), x1, gate2, gpost, nxt)


def kernel(x, c, ctx, c_ctx, ada_w, ada_b, norm_g, attn_w_in, attn_w_out, na_rpb, q_norm_g, k_norm_g, mix_w_in, mix_w_out, conv_w, sgu_w, sgu_b, sgu_norm_g, router_w, router_bias, moe_w_gate, moe_w_up, moe_w_down, shared_w_gate, shared_w_up, shared_w_down):
    nb, s, d = x.shape
    assert (s, d) == (SEQ, D_MODEL) and ctx.shape == (nb, CTX_LEN, d)
    x_lat = x.reshape(nb * s, d)
    x_ctx = ctx.reshape(nb * CTX_LEN, d)

    c_rows = jnp.concatenate([c, c_ctx[None, :], jnp.zeros((16 - nb - 1, d), F32)], axis=0)
    mods = _adaln(c_rows, ada_w, ada_b)
    lat_mods = [_mod_parts(mods[i], nb) for i in range(DEPTH)]
    ctx_mods = _mod_parts(mods[0, nb:nb + 1], 1)

    cos, sin = _rope_tables()
    bands, pats, na_idx = _na_static_tables()

    sh1, sc1 = lat_mods[0][0], lat_mods[0][1]
    h_lat = _norm_mod(x_lat, norm_g[0, 0], sh1, sc1, SEQ)

    for i in range(DEPTH):
        _, _, g1, sh2, sc2, g2 = lat_mods[i]
        if i % 2 == 0:
            e = i // 2
            w_in = attn_w_in[e]
            kv_lo, kv_hi = NA_DIM, 3 * NA_DIM
            gk_lo = 3 * NA_DIM + GQA_Q_DIM
            h_ctx = _norm_mod(x_ctx, norm_g[i, 0], ctx_mods[0], ctx_mods[1], nb * CTX_LEN)
            w_ctx = jnp.concatenate([w_in[:, kv_lo:kv_hi], w_in[:, gk_lo:]], axis=1).astype(BF16)
            p_lat = _matmul(h_lat, w_in.astype(BF16))
            p_ctx = _matmul(h_ctx, w_ctx)
            blk = 4 * HEAD_DIM
            qg = _qk_prep(p_lat, kv_hi // blk, GQA_Q_DIM // blk, q_norm_g[e], cos, sin, True)
            kg = _qk_prep(p_lat, gk_lo // blk, GQA_KV_DIM // blk, k_norm_g[e], cos, sin, True)
            kcg = _qk_prep(p_ctx, 2 * NA_DIM // blk, GQA_KV_DIM // blk, k_norm_g[e], cos, sin, False)
            bias = _na_bias_table(na_rpb[e], na_idx)
            o_na = _na_attention(p_lat, p_ctx, bias, bands, pats)
            o_gqa = _gqa_attention(qg, kg, p_lat, kcg, p_ctx,
                                   (gk_lo + GQA_KV_DIM) // HEAD_DIM, (2 * NA_DIM + GQA_KV_DIM) // HEAD_DIM)
            parts = [o_na, o_gqa]
            w_out = attn_w_out[e].astype(BF16)
        else:
            o = i // 2
            p_mix = _matmul(h_lat, mix_w_in[o].astype(BF16))
            parts = [_conv_sgu(p_mix, conv_w[o], sgu_w[o], sgu_b[o], sgu_norm_g[o])]
            wo = mix_w_out[o]
            w_out = jnp.concatenate(
                [wo[:CONV_DIM].reshape(SGU_GROUPS, SGU_CH, d), wo[CONV_DIM:].reshape(SGU_GROUPS, SGU_CH, d)],
                axis=1).reshape(CONV_DIM + SGU_DIM, d).astype(BF16)
        x1, h2, h2_rec = _matmul_post(parts, w_out, x_lat, g1, norm_g[i, 1], norm_g[i, 2], sh2, sc2)
        nxt = None
        if i + 1 < DEPTH:
            nxt = (norm_g[i + 1, 0], lat_mods[i + 1][0], lat_mods[i + 1][1])
        x_lat, h_lat = _moe(h2, h2_rec, x1, g2, norm_g[i, 3], nxt, router_w[i], router_bias[i],
                            i, moe_w_gate, moe_w_up, moe_w_down,
                            shared_w_gate[i], shared_w_up[i], shared_w_down[i])
    return x_lat.reshape(nb, s, d)
```

```python
import functools

import numpy as np
import jax
import jax.numpy as jnp
from jax import lax
from jax.experimental import pallas as pl
from jax.experimental.pallas import tpu as pltpu

F32 = jnp.float32
BF16 = jnp.bfloat16

D_MODEL = 4096
SEQ = 2048
DEPTH = 2
GRID_W = 64
GRID_ROWS = SEQ // GRID_W
CTX_LEN = 256
HEAD_DIM = 128
N_NA_HEADS = 16
N_GQA_HEADS = 16
N_GQA_KV = 4
GQA_REP = N_GQA_HEADS // N_GQA_KV
NA_KH = 8
NA_KW = 16
NA_BAND_ROWS = NA_KH + 1
NA_KEYS = NA_BAND_ROWS * GRID_W
NA_QBLK = 2 * GRID_W
NA_NBLK = GRID_ROWS // 2
ROPE_THETA = 10000.0
CONV_DIM = D_MODEL // 2
SGU_DIM = D_MODEL // 2
SGU_GROUPS = 8
SGU_CH = SGU_DIM // SGU_GROUPS
SGU_CHUNK = 128
N_EXPERTS = 64
N_EXPERT_GROUPS = 8
EXPERTS_PER_GROUP = N_EXPERTS // N_EXPERT_GROUPS
TOPK_GROUPS = 4
TOP_K = 8
D_EXPERT = D_MODEL // 16
ROUTED_SCALE = 2.5
EPS = 1e-6
NEG = -1e30
NA_DIM = N_NA_HEADS * HEAD_DIM
GQA_Q_DIM = N_GQA_HEADS * HEAD_DIM
GQA_KV_DIM = N_GQA_KV * HEAD_DIM
ATTN_SCALE = HEAD_DIM ** -0.5
LANES = 128
REC_ROWS = D_MODEL // 2 // LANES
MOE_TILE = 256
RECORD_PITCH = 20
COMBINE_TOKENS = 64
EXPERT_RING = 3
EXPERT_GATHER_QUEUES = (1,)
COMBINE_GATHER_QUEUES = (0, 1)

V7X_VMEM_BYTES = 64 * 1024 * 1024
MIB = 1024 * 1024


def _cparams(semantics, vmem_mib):
    assert vmem_mib * MIB < V7X_VMEM_BYTES
    return pltpu.CompilerParams(dimension_semantics=semantics, vmem_limit_bytes=vmem_mib * MIB)


def _rms_scale(x):
    return lax.rsqrt(jnp.mean(x * x, axis=-1, keepdims=True) + EPS)


def _silu(x):
    return x / (1.0 + jnp.exp(-x))


def _adaln_kernel(c_ref, w_ref, b_ref, o_ref):
    a = _silu(c_ref[...])
    o_ref[0] = jnp.dot(a.astype(BF16), w_ref[0].astype(BF16), preferred_element_type=F32) + b_ref[0]


def _adaln(c_rows, ada_w, ada_b):
    depth, d, n6 = ada_w.shape
    rows = c_rows.shape[0]
    tn = 512
    return pl.pallas_call(
        _adaln_kernel,
        grid=(depth, n6 // tn),
        in_specs=[pl.BlockSpec((rows, d), lambda l, j: (0, 0)),
                  pl.BlockSpec((1, d, tn), lambda l, j: (l, 0, j)),
                  pl.BlockSpec((1, 1, tn), lambda l, j: (l, 0, j))],
        out_specs=pl.BlockSpec((1, rows, tn), lambda l, j: (l, 0, j)),
        out_shape=jax.ShapeDtypeStruct((depth, rows, n6), F32),
        compiler_params=_cparams(("parallel", "parallel"), 40),
        name="adaln",
    )(c_rows, ada_w, ada_b.reshape(depth, 1, n6))


def _norm_mod_kernel(x_ref, g_ref, sh_ref, sc_ref, o_ref):
    x = x_ref[...]
    y = (x * _rms_scale(x)) * g_ref[...]
    o_ref[...] = (y * (1.0 + sc_ref[0]) + sh_ref[0]).astype(o_ref.dtype)


def _norm_mod(x, g, sh, sc, rows_per_mod):
    m, d = x.shape
    tm = 256
    mod_spec = pl.BlockSpec((1, 1, d), lambda i: (i * tm // rows_per_mod, 0, 0))
    return pl.pallas_call(
        _norm_mod_kernel,
        grid=(m // tm,),
        in_specs=[pl.BlockSpec((tm, d), lambda i: (i, 0)),
                  pl.BlockSpec((1, d), lambda i: (0, 0)),
                  mod_spec, mod_spec],
        out_specs=pl.BlockSpec((tm, d), lambda i: (i, 0)),
        out_shape=jax.ShapeDtypeStruct((m, d), BF16),
        compiler_params=_cparams(("parallel",), 40),
        name="norm_mod",
    )(x, g.reshape(1, d), sh, sc)


def _mm_kernel(x_ref, w_ref, o_ref):
    o_ref[...] = jnp.dot(x_ref[...], w_ref[...], preferred_element_type=F32).astype(o_ref.dtype)


def _matmul(x, w, tm=1024, tn=512):
    m, k = x.shape
    n = w.shape[1]
    tm = min(tm, m)
    return pl.pallas_call(
        _mm_kernel,
        grid=(m // tm, n // tn),
        in_specs=[pl.BlockSpec((tm, k), lambda i, j: (i, 0)),
                  pl.BlockSpec((k, tn), lambda i, j: (0, j))],
        out_specs=pl.BlockSpec((tm, tn), lambda i, j: (i, j)),
        out_shape=jax.ShapeDtypeStruct((m, n), BF16),
        compiler_params=_cparams(("parallel", "arbitrary"), 48),
        name="matmul",
    )(x, w)


def _pack_bf16_pairs(lo, hi):
    return pltpu.pack_elementwise([lo, hi], packed_dtype=BF16)


def _unpack_bf16_pairs(words):
    lo = pltpu.unpack_elementwise(words, index=0, packed_dtype=BF16, unpacked_dtype=F32)
    hi = pltpu.unpack_elementwise(words, index=1, packed_dtype=BF16, unpacked_dtype=F32)
    return lo, hi


def _store_records(ref, first_row, words, pitch):
    n = words.shape[0]
    for s in range(REC_ROWS):
        ref[pl.ds(first_row + s, n, stride=pitch), :] = words[:, s * LANES:(s + 1) * LANES]
    for s in range(REC_ROWS, pitch):
        ref[pl.ds(first_row + s, n, stride=pitch), :] = jnp.zeros((n, LANES), words.dtype)


def _load_records(ref, first_row, n, pitch):
    return jnp.concatenate([ref[pl.ds(first_row + s, n, stride=pitch), :] for s in range(REC_ROWS)], axis=1)


def _mm_post_kernel(*refs, n_parts, nk, n_chunks, row_chunk):
    lhs = refs[:n_parts]
    ws = refs[n_parts:2 * n_parts]
    xres, gate, gpost, gnext, sh, sc, xout, hout, hrec, vecs = refs[2 * n_parts:]
    k = pl.program_id(1)
    tm, d = xout.shape
    tn = d // n_chunks

    def part(c):
        acc = None
        for l, w in zip(lhs, ws):
            t = jnp.dot(l[...], w[:, c * tn:(c + 1) * tn], preferred_element_type=F32)
            acc = t if acc is None else acc + t
        return acc

    @pl.when(k == 0)
    def _():
        for c in range(n_chunks):
            xout[:, c * tn:(c + 1) * tn] = part(c)

    @pl.when(k > 0)
    def _():
        for c in range(n_chunks):
            xout[:, c * tn:(c + 1) * tn] += part(c)

    @pl.when(k == nk - 1)
    def _():
        vecs[0:1, :] = gate[0] * gpost[...]
        vecs[1:2, :] = gnext[...] * (1.0 + sc[0])

        def body(r, carry):
            rows = pl.ds(pl.multiple_of(r * row_chunk, row_chunk), row_chunk)
            y = xout[rows, :]
            xn = xres[rows, :] + (y * _rms_scale(y)) * vecs[0:1, :]
            xout[rows, :] = xn
            h = (xn * _rms_scale(xn)) * vecs[1:2, :] + sh[0]
            hout[rows, :] = h.astype(hout.dtype)
            half = d // 2
            first = pl.multiple_of(r * (row_chunk * RECORD_PITCH), row_chunk * RECORD_PITCH)
            _store_records(hrec, first, _pack_bf16_pairs(h[:, :half], h[:, half:]), RECORD_PITCH)
            return carry
        lax.fori_loop(0, tm // row_chunk, body, 0)


def _matmul_post(lhs_parts, w, xres, gate, gpost, gnext, sh, sc, tm=512):
    m, d = xres.shape
    n_parts = len(lhs_parts)
    kp = lhs_parts[0].shape[1]
    assert all(l.shape == (m, kp) for l in lhs_parts) and w.shape == (n_parts * kp, d)
    tk = 512 // n_parts
    nk = kp // tk
    vec_spec = pl.BlockSpec((1, d), lambda i, k: (0, 0))
    mod_spec = pl.BlockSpec((1, 1, d), lambda i, k: (i * tm // SEQ, 0, 0))
    row_spec = pl.BlockSpec((tm, d), lambda i, k: (i, 0))
    in_specs = [pl.BlockSpec((tm, tk), lambda i, k: (i, k)) for _ in range(n_parts)]
    in_specs += [pl.BlockSpec((tk, d), functools.partial(lambda i, k, off: (k + off, 0), off=p * nk))
                 for p in range(n_parts)]
    in_specs += [pl.BlockSpec((tm, d), lambda i, k: (i, 0), pipeline_mode=pl.Buffered(1)),
                 mod_spec, vec_spec, vec_spec, mod_spec, mod_spec]
    args = list(lhs_parts) + [w] * n_parts + [xres, gate, gpost.reshape(1, d), gnext.reshape(1, d), sh, sc]
    return pl.pallas_call(
        functools.partial(_mm_post_kernel, n_parts=n_parts, nk=nk, n_chunks=4, row_chunk=32),
        grid=(m // tm, nk),
        in_specs=in_specs,
        out_specs=[row_spec, row_spec, pl.BlockSpec((tm * RECORD_PITCH, LANES), lambda i, k: (i, 0))],
        out_shape=[jax.ShapeDtypeStruct((m, d), F32), jax.ShapeDtypeStruct((m, d), BF16),
                   jax.ShapeDtypeStruct((m * RECORD_PITCH, LANES), jnp.int32)],
        scratch_shapes=[pltpu.VMEM((8, d), F32)],
        compiler_params=_cparams(("parallel", "arbitrary"), 56),
        name="matmul_post",
    )(*args)


def _qk_prep_kernel(x_ref, g_ref, cos_ref, sin_ref, o_ref, *, rope):
    n_heads = x_ref.shape[1] // HEAD_DIM
    for hh in range(n_heads):
        cols = slice(hh * HEAD_DIM, (hh + 1) * HEAD_DIM)
        x = x_ref[:, cols].astype(F32)
        y = (x * _rms_scale(x)) * g_ref[...]
        if rope:
            lane = lax.broadcasted_iota(jnp.int32, y.shape, 1)
            quarter = HEAD_DIM // 4
            partner = jnp.where(jnp.bitwise_and(lane, 2 * quarter - 1) < quarter,
                                pltpu.roll(y, HEAD_DIM - quarter, axis=1), pltpu.roll(y, quarter, axis=1))
            y = y * cos_ref[...] + partner * sin_ref[...]
        o_ref[:, cols] = y.astype(o_ref.dtype)


def _qk_prep(p, col_block0, n_col_blocks, g, cos, sin, rope):
    m = p.shape[0]
    tm, tc = 512, 4 * HEAD_DIM
    pos_blocks = SEQ // tm
    return pl.pallas_call(
        functools.partial(_qk_prep_kernel, rope=rope),
        grid=(m // tm, n_col_blocks),
        in_specs=[pl.BlockSpec((tm, tc), lambda i, j: (i, col_block0 + j)),
                  pl.BlockSpec((1, HEAD_DIM), lambda i, j: (0, 0)),
                  pl.BlockSpec((tm, HEAD_DIM), lambda i, j: (i % pos_blocks, 0)),
                  pl.BlockSpec((tm, HEAD_DIM), lambda i, j: (i % pos_blocks, 0))],
        out_specs=pl.BlockSpec((tm, tc), lambda i, j: (i, j)),
        out_shape=jax.ShapeDtypeStruct((m, tc * n_col_blocks), BF16),
        compiler_params=_cparams(("parallel", "parallel"), 32),
        name="qk_prep",
    )(p, g.reshape(1, HEAD_DIM), cos, sin)


def _rope_tables():
    half = HEAD_DIM // 4
    t = jnp.arange(SEQ)
    freqs = ROPE_THETA ** (-jnp.arange(half, dtype=F32) / half)
    ang_r = (t // GRID_W).astype(F32)[:, None] * freqs[None, :]
    ang_c = (t % GRID_W).astype(F32)[:, None] * freqs[None, :]
    cos = jnp.concatenate([jnp.cos(ang_r)] * 2 + [jnp.cos(ang_c)] * 2, axis=-1)
    sin = jnp.concatenate([-jnp.sin(ang_r), jnp.sin(ang_r), -jnp.sin(ang_c), jnp.sin(ang_c)], axis=-1)
    return cos, sin


def _na_static_tables():
    rows, kh, kw = GRID_ROWS, min(NA_KH, GRID_ROWS), NA_KW
    nb_rows = min(kh + 1, rows)
    assert nb_rows == NA_BAND_ROWS
    col = np.arange(GRID_W)
    col_start = np.clip(col - kw // 2, 0, GRID_W - kw)
    patterns, pat_of_block, bands = [], [], []
    for b in range(NA_NBLK):
        q_rows = 2 * b + np.arange(2)
        row_start = np.clip(q_rows - kh // 2, 0, rows - kh)
        band = int(np.clip(row_start[0], 0, rows - nb_rows))
        key = (tuple(q_rows - band), tuple(row_start - band))
        if key not in patterns:
            patterns.append(key)
        pat_of_block.append(patterns.index(key))
        bands.append(band)
    idx = np.full((len(patterns), NA_QBLK, NA_KEYS), -1, np.int32)
    for p, (q_rel, rs_rel) in enumerate(patterns):
        qr = np.repeat(np.array(q_rel), GRID_W)
        qc = np.tile(col, 2)
        rs = np.repeat(np.array(rs_rel), GRID_W)
        cs = np.tile(col_start, 2)
        kr = np.repeat(np.arange(nb_rows), GRID_W)
        kc = np.tile(col, nb_rows)
        in_win = ((kr[None, :] >= rs[:, None]) & (kr[None, :] < rs[:, None] + kh)
                  & (kc[None, :] >= cs[:, None]) & (kc[None, :] < cs[:, None] + kw))
        dr = np.clip(kr[None, :] - qr[:, None] + NA_KH - 1, 0, 2 * NA_KH - 2)
        dc = np.clip(kc[None, :] - qc[:, None] + NA_KW - 1, 0, 2 * NA_KW - 2)
        idx[p] = np.where(in_win, dr * (2 * NA_KW - 1) + dc, -1)
    return np.array(bands, np.int32), np.array(pat_of_block, np.int32), idx


def _na_bias_kernel(idx_ref, rpb_ref, o_ref):
    idx = idx_ref[0]
    n_tab = rpb_ref.shape[1]
    onehot = (lax.broadcasted_iota(jnp.int32, (n_tab, idx.shape[1]), 0) == idx).astype(BF16)
    r = rpb_ref[...]
    hi = r.astype(BF16)
    r1 = r - hi.astype(F32)
    mid = r1.astype(BF16)
    lo = (r1 - mid.astype(F32)).astype(BF16)
    val = (jnp.dot(hi, onehot, preferred_element_type=F32) + jnp.dot(mid, onehot, preferred_element_type=F32)
           + jnp.dot(lo, onehot, preferred_element_type=F32))
    o_ref[0] = jnp.where(idx < 0, NEG, val)


def _na_bias_table(rpb, idx):
    n_pat = idx.shape[0]
    h = rpb.shape[0]
    n_tab = 512
    flat = NA_QBLK * NA_KEYS
    tile = 8 * NA_KEYS
    rpb_flat = jnp.pad(rpb.reshape(h, -1), ((0, 0), (0, n_tab - rpb.shape[1] * rpb.shape[2])))
    out = pl.pallas_call(
        _na_bias_kernel,
        grid=(n_pat, flat // tile),
        in_specs=[pl.BlockSpec((1, 1, tile), lambda p, j: (p, 0, j)),
                  pl.BlockSpec((h, n_tab), lambda p, j: (0, 0))],
        out_specs=pl.BlockSpec((1, h, tile), lambda p, j: (p, 0, j)),
        out_shape=jax.ShapeDtypeStruct((n_pat, h, flat), F32),
        compiler_params=_cparams(("parallel", "parallel"), 32),
        name="na_bias",
    )(jnp.asarray(idx.reshape(n_pat, 1, flat)), rpb_flat)
    return out.reshape(n_pat, h, NA_QBLK, NA_KEYS)


_NT = (((1,), (1,)), ((), ()))


def _na_attn_kernel(q_ref, k_ref, v_ref, kc_ref, vc_ref, bias_ref, o_ref, *, bands, pats):
    kc = kc_ref[...]
    vc = vc_ref[...]

    for n in range(NA_NBLK):
        q_rows = slice(n * NA_QBLK, (n + 1) * NA_QBLK)
        key_rows = slice(bands[n] * GRID_W, bands[n] * GRID_W + NA_KEYS)
        q = q_ref[q_rows, :]
        bias = bias_ref[pats[n], 0]
        s = lax.dot_general(q, k_ref[key_rows, :], _NT, preferred_element_type=F32) * ATTN_SCALE + bias
        s = jnp.where(bias <= 0.5 * NEG, NEG, s)
        s_ctx = lax.dot_general(q, kc, _NT, preferred_element_type=F32) * ATTN_SCALE
        m = jnp.maximum(jnp.max(s, axis=-1, keepdims=True), jnp.max(s_ctx, axis=-1, keepdims=True))
        p = jnp.exp(s - m)
        p_ctx = jnp.exp(s_ctx - m)
        denom = jnp.sum(p, axis=-1, keepdims=True) + jnp.sum(p_ctx, axis=-1, keepdims=True)
        o = (jnp.dot(p.astype(BF16), v_ref[key_rows, :], preferred_element_type=F32)
             + jnp.dot(p_ctx.astype(BF16), vc, preferred_element_type=F32))
        o_ref[q_rows, :] = (o / denom).astype(o_ref.dtype)


def _na_attention(p_lat, p_ctx, bias, bands, pats):
    n_tok = p_lat.shape[0]
    nb = n_tok // SEQ
    n_pat = bias.shape[0]
    hd = HEAD_DIM
    return pl.pallas_call(
        functools.partial(_na_attn_kernel, bands=tuple(int(b) for b in bands), pats=tuple(int(p) for p in pats)),
        grid=(N_NA_HEADS, nb),
        in_specs=[pl.BlockSpec((SEQ, hd), lambda h, b: (b, h)),
                  pl.BlockSpec((SEQ, hd), lambda h, b: (b, N_NA_HEADS + h)),
                  pl.BlockSpec((SEQ, hd), lambda h, b: (b, 2 * N_NA_HEADS + h)),
                  pl.BlockSpec((CTX_LEN, hd), lambda h, b: (b, h)),
                  pl.BlockSpec((CTX_LEN, hd), lambda h, b: (b, N_NA_HEADS + h)),
                  pl.BlockSpec((n_pat, 1, NA_QBLK, NA_KEYS), lambda h, b: (0, h, 0, 0))],
        out_specs=pl.BlockSpec((SEQ, hd), lambda h, b: (b, h)),
        out_shape=jax.ShapeDtypeStruct((n_tok, NA_DIM), BF16),
        compiler_params=_cparams(("parallel", "parallel"), 32),
        name="na_attention",
    )(p_lat, p_lat, p_lat, p_ctx, p_ctx, bias)


def _gqa_attn_kernel(q_ref, k_ref, v_ref, kc_ref, vc_ref, o_ref):
    k = k_ref[...]
    v = v_ref[...]
    kc = kc_ref[...]
    vc = vc_ref[...]
    for r in range(GQA_REP):
        cols = slice(r * HEAD_DIM, (r + 1) * HEAD_DIM)
        q = q_ref[:, cols]
        s = lax.dot_general(q, k, _NT, preferred_element_type=F32) * ATTN_SCALE
        s_ctx = lax.dot_general(q, kc, _NT, preferred_element_type=F32) * ATTN_SCALE
        m = jnp.maximum(jnp.max(s, axis=-1, keepdims=True), jnp.max(s_ctx, axis=-1, keepdims=True))
        p = jnp.exp(s - m)
        p_ctx = jnp.exp(s_ctx - m)
        denom = jnp.sum(p, axis=-1, keepdims=True) + jnp.sum(p_ctx, axis=-1, keepdims=True)
        o = (jnp.dot(p.astype(BF16), v, preferred_element_type=F32)
             + jnp.dot(p_ctx.astype(BF16), vc, preferred_element_type=F32))
        o_ref[:, cols] = (o / denom).astype(o_ref.dtype)


def _gqa_attention(qg, kg, p_lat, kcg, p_ctx, v_col_block, vc_col_block, tq=256):
    n_tok = qg.shape[0]
    nb = n_tok // SEQ
    hd = HEAD_DIM
    qt = SEQ // tq
    return pl.pallas_call(
        _gqa_attn_kernel,
        grid=(nb, N_GQA_KV, qt),
        in_specs=[pl.BlockSpec((tq, GQA_REP * hd), lambda b, g, t: (b * qt + t, g)),
                  pl.BlockSpec((SEQ, hd), lambda b, g, t: (b, g)),
                  pl.BlockSpec((SEQ, hd), lambda b, g, t: (b, v_col_block + g)),
                  pl.BlockSpec((CTX_LEN, hd), lambda b, g, t: (b, g)),
                  pl.BlockSpec((CTX_LEN, hd), lambda b, g, t: (b, vc_col_block + g))],
        out_specs=pl.BlockSpec((tq, GQA_REP * hd), lambda b, g, t: (b * qt + t, g)),
        out_shape=jax.ShapeDtypeStruct((n_tok, GQA_Q_DIM), BF16),
        compiler_params=_cparams(("parallel", "parallel", "parallel"), 48),
        name="gqa_attention",
    )(qg, kg, p_lat, kcg, p_ctx)


def _mixer_kernel(gb_ref, gc_ref, xc_ref, u_ref, v_ref, cw_ref, sw_ref, sb_ref, sg_ref, o_ref):
    s, c = gb_ref.shape
    z = gc_ref[...].astype(F32) * xc_ref[...].astype(F32)
    row = lax.broadcasted_iota(jnp.int32, (s, c), 0)
    z_prev = jnp.where(row == 0, 0.0, pltpu.roll(z, 1, axis=0))
    z_next = jnp.where(row == s - 1, 0.0, pltpu.roll(z, s - 1, axis=0))
    cw = cw_ref[...]
    conv = z_prev * cw[0:1, :] + z * cw[1:2, :] + z_next * cw[2:3, :]
    o_ref[:, :c] = (gb_ref[...].astype(F32) * conv).astype(o_ref.dtype)
    w = sw_ref[0].astype(BF16)
    for n in range(s // SGU_CHUNK):
        rows = slice(n * SGU_CHUNK, (n + 1) * SGU_CHUNK)
        vv = v_ref[rows, :].astype(F32)
        vg = (vv * _rms_scale(vv)) * sg_ref[0]
        mix = jnp.dot(w, vg.astype(BF16), preferred_element_type=F32) + sb_ref[0]
        o_ref[rows, c:] = (u_ref[rows, :].astype(F32) * mix).astype(o_ref.dtype)


def _conv_sgu(p, conv_w, sgu_w, sgu_b, sgu_g):
    n_tok = p.shape[0]
    nb = n_tok // SEQ
    c = SGU_CH
    blocks = CONV_DIM // c
    col = lambda off: pl.BlockSpec((SEQ, c), functools.partial(lambda b, g, off: (b, off + g), off=off))
    return pl.pallas_call(
        _mixer_kernel,
        grid=(nb, SGU_GROUPS),
        in_specs=[col(0), col(blocks), col(2 * blocks), col(3 * blocks), col(4 * blocks),
                  pl.BlockSpec((3, c), lambda b, g: (0, g)),
                  pl.BlockSpec((1, SGU_CHUNK, SGU_CHUNK), lambda b, g: (g, 0, 0)),
                  pl.BlockSpec((1, SGU_CHUNK, 1), lambda b, g: (g, 0, 0)),
                  pl.BlockSpec((1, 1, c), lambda b, g: (g, 0, 0))],
        out_specs=pl.BlockSpec((SEQ, 2 * c), lambda b, g: (b, g)),
        out_shape=jax.ShapeDtypeStruct((n_tok, CONV_DIM + SGU_DIM), BF16),
        compiler_params=_cparams(("parallel", "parallel"), 48),
        name="conv_sgu",
    )(p, p, p, p, p, conv_w, sgu_w, sgu_b.reshape(SGU_GROUPS, SGU_CHUNK, 1), sgu_g.reshape(SGU_GROUPS, 1, c))


def _router_kernel(x_ref, whi_ref, wlo_ref, bias_ref, tri_ref, wsh_ref,
                   idx_ref, wgt_ref, pos_ref, cnt_ref, hs_ref, carry_ref):
    @pl.when(pl.program_id(0) == 0)
    def _():
        carry_ref[...] = jnp.zeros_like(carry_ref)

    x = x_ref[...]
    hsu = jnp.dot(x, wsh_ref[...], preferred_element_type=F32)
    f = hsu.shape[1] // 2
    hs_ref[...] = (_silu(hsu[:, :f]) * hsu[:, f:]).astype(hs_ref.dtype)

    logits = (lax.dot_general(whi_ref[...], x, _NT, preferred_element_type=F32)
              + lax.dot_general(wlo_ref[...], x, _NT, preferred_element_type=F32))
    scores = 1.0 / (1.0 + jnp.exp(-logits))
    sel = scores + bias_ref[...]
    n_e, tm = sel.shape
    epg = EXPERTS_PER_GROUP
    groups = [sel[g * epg:(g + 1) * epg, :] for g in range(N_EXPERT_GROUPS)]
    gscore = []
    for sg in groups:
        top1 = jnp.max(sg, axis=0, keepdims=True)
        is_top = sg == top1
        n_top = jnp.sum(is_top.astype(F32), axis=0, keepdims=True)
        second = jnp.max(jnp.where(is_top, -jnp.inf, sg), axis=0, keepdims=True)
        gscore.append(top1 + jnp.where(n_top >= 2.0, top1, second))
    masked = []
    for g in range(N_EXPERT_GROUPS):
        rank = jnp.zeros((1, tm), F32)
        for o in range(N_EXPERT_GROUPS):
            if o == g:
                continue
            beats = (gscore[o] >= gscore[g]) if o < g else (gscore[o] > gscore[g])
            rank = rank + beats.astype(F32)
        masked.append(jnp.where(rank < float(TOPK_GROUPS), groups[g], -jnp.inf))
    v = jnp.concatenate(masked, axis=0)
    row = lax.broadcasted_iota(jnp.int32, (n_e, tm), 0)
    chosen = jnp.zeros((n_e, tm), F32)
    picks, top_s = [], []
    for _ in range(TOP_K):
        best = jnp.max(v, axis=0, keepdims=True)
        first = jnp.min(jnp.where(v == best, row, n_e), axis=0, keepdims=True)
        pick = row == first
        picks.append(first)
        top_s.append(jnp.sum(jnp.where(pick, scores, 0.0), axis=0, keepdims=True))
        chosen = jnp.where(pick, 1.0, chosen)
        v = jnp.where(pick, -jnp.inf, v)
    total = top_s[0]
    for s in top_s[1:]:
        total = total + s
    running = jnp.dot(chosen.astype(BF16), tri_ref[...], preferred_element_type=F32) + carry_ref[...]
    pos = [jnp.sum(jnp.where(row == first, running - 1.0, 0.0), axis=0, keepdims=True) for first in picks]
    idx_ref[...] = jnp.concatenate(picks, axis=0)
    wgt_ref[...] = jnp.concatenate([s / total * ROUTED_SCALE for s in top_s], axis=0)
    pos_ref[...] = jnp.concatenate(pos, axis=0).astype(jnp.int32)
    carry_ref[...] += jnp.sum(chosen, axis=1, keepdims=True)
    cnt_ref[...] = carry_ref[...].astype(jnp.int32)


def _router(h, router_w, router_bias, sh_gate, sh_up, tm=1024):
    n, d = h.shape
    f = sh_gate.shape[1]
    wt = router_w.T
    w_hi = wt.astype(BF16)
    w_lo = (wt - w_hi.astype(F32)).astype(BF16)
    tri = (jnp.arange(tm)[:, None] <= jnp.arange(tm)[None, :]).astype(BF16)
    w_sh = jnp.concatenate([sh_gate, sh_up], axis=1).astype(BF16)
    fixed = lambda shape: pl.BlockSpec(shape, lambda i: (0, 0))
    per_tok = pl.BlockSpec((TOP_K, tm), lambda i: (0, i))
    return pl.pallas_call(
        _router_kernel,
        grid=(n // tm,),
        in_specs=[pl.BlockSpec((tm, d), lambda i: (i, 0)), fixed((N_EXPERTS, d)), fixed((N_EXPERTS, d)),
                  fixed((N_EXPERTS, 1)), fixed((tm, tm)), fixed((d, 2 * f))],
        out_specs=[per_tok, per_tok, per_tok, fixed((N_EXPERTS, 1)), pl.BlockSpec((tm, f), lambda i: (i, 0))],
        out_shape=[jax.ShapeDtypeStruct((TOP_K, n), jnp.int32), jax.ShapeDtypeStruct((TOP_K, n), F32),
                   jax.ShapeDtypeStruct((TOP_K, n), jnp.int32), jax.ShapeDtypeStruct((N_EXPERTS, 1), jnp.int32),
                   jax.ShapeDtypeStruct((n, f), BF16)],
        scratch_shapes=[pltpu.VMEM((N_EXPERTS, 1), F32)],
        compiler_params=_cparams(("arbitrary",), 48),
        name="router",
    )(h, w_hi, w_lo, router_bias.reshape(N_EXPERTS, 1), tri, w_sh)


def _start_record_gather(idx_ref, n, src_hbm, dst, sem, pitch, priorities, unroll=8):
    def body(j, carry):
        for u in range(unroll):
            r = j * unroll + u
            src_row = pl.multiple_of(idx_ref[0, 0, r] * pitch, 4)
            dst_row = pl.multiple_of(r * pitch, 4)
            pltpu.make_async_copy(src_hbm.at[pl.ds(src_row, REC_ROWS), :], dst.at[pl.ds(dst_row, REC_ROWS), :],
                                  sem).start(priority=priorities[u % len(priorities)])
        return carry
    lax.fori_loop(0, n // unroll, body, 0)


def _wait_record_gather(n, src_hbm, dst, sem):
    pltpu.make_async_copy(src_hbm.at[pl.ds(0, n * REC_ROWS), :], dst.at[pl.ds(0, n * REC_ROWS), :], sem).wait()


def _issue_records(idx_ref, r0, r1, src_hbm, dst, sem, pitch, priorities):
    for r in range(r0, r1):
        src_row = pl.multiple_of(idx_ref[0, 0, r] * pitch, 4)
        pltpu.make_async_copy(src_hbm.at[pl.ds(src_row, REC_ROWS), :], dst.at[pl.ds(r * pitch, REC_ROWS), :],
                              sem).start(priority=priorities[r % len(priorities)])


def _moe_expert_kernel(te_ref, tok_ref, tok_next_ref, tok_ahead_ref, x_hbm, wg_ref, wu_ref, wd_ref, y_ref,
                       xbuf, sem, wg_bf, wu_bf, wd_bf):
    i = pl.program_id(0)
    last = pl.num_programs(0) - 1
    slot = lax.rem(i, EXPERT_RING)
    slot_next = lax.rem(i + 1, EXPERT_RING)
    slot_ahead = lax.rem(i + 2, EXPERT_RING)
    tm = tok_ref.shape[2]
    half = D_MODEL // 2
    chunk = 4 * LANES
    n_chunks = half // chunk

    @pl.when(i == 0)
    def _():
        _start_record_gather(tok_ref, tm, x_hbm, xbuf.at[0], sem.at[0], RECORD_PITCH, EXPERT_GATHER_QUEUES)
        _start_record_gather(tok_next_ref, tm, x_hbm, xbuf.at[1], sem.at[1], RECORD_PITCH, EXPERT_GATHER_QUEUES)

    @pl.when(jnp.logical_or(i == 0, te_ref[i] != te_ref[jnp.maximum(i - 1, 0)]))
    def _():
        wg_bf[...] = wg_ref[0, 0].astype(BF16)
        wu_bf[...] = wu_ref[0, 0].astype(BF16)
        wd_bf[...] = wd_ref[0, 0].astype(BF16)

    _wait_record_gather(tm, x_hbm, xbuf.at[slot], sem.at[slot])
    lo, hi = _unpack_bf16_pairs(_load_records(xbuf.at[slot], 0, tm, RECORD_PITCH))
    lo = lo.astype(BF16)
    hi = hi.astype(BF16)
    n_groups = n_chunks + 2
    bounds = [tm * g // n_groups for g in range(n_groups + 1)]
    prefetch = lambda g: _issue_records(tok_ahead_ref, bounds[g], bounds[g + 1], x_hbm, xbuf.at[slot_ahead],
                                        sem.at[slot_ahead], RECORD_PITCH, EXPERT_GATHER_QUEUES)
    prefetch(0)
    pre_g = (jnp.dot(lo, wg_bf[:half, :], preferred_element_type=F32)
             + jnp.dot(hi, wg_bf[half:, :], preferred_element_type=F32))
    prefetch(1)
    pre_u = (jnp.dot(lo, wu_bf[:half, :], preferred_element_type=F32)
             + jnp.dot(hi, wu_bf[half:, :], preferred_element_type=F32))
    hid = (_silu(pre_g) * pre_u).astype(BF16)
    for c in range(n_chunks):
        prefetch(2 + c)
        y_lo = jnp.dot(hid, wd_bf[:, c * chunk:(c + 1) * chunk], preferred_element_type=F32)
        y_hi = jnp.dot(hid, wd_bf[:, half + c * chunk:half + (c + 1) * chunk], preferred_element_type=F32)
        words = _pack_bf16_pairs(y_lo, y_hi)
        for s in range(chunk // LANES):
            y_ref[pl.ds(c * (chunk // LANES) + s, tm, stride=RECORD_PITCH), :] = words[:, s * LANES:(s + 1) * LANES]
    for s in range(REC_ROWS, RECORD_PITCH):
        y_ref[pl.ds(s, tm, stride=RECORD_PITCH), :] = jnp.zeros((tm, LANES), y_ref.dtype)

    @pl.when(i == last)
    def _():
        _wait_record_gather(tm, x_hbm, xbuf.at[slot_next], sem.at[slot_next])
        _wait_record_gather(tm, x_hbm, xbuf.at[slot_ahead], sem.at[slot_ahead])


def _moe_experts(x_rec, tile_expert, tok_slot, layer, w_gate, w_up, w_down):
    t_max, _, tm = tok_slot.shape
    _, e, d, f = w_gate.shape
    tok_spec = lambda shift: pl.BlockSpec(
        (1, 1, tm), functools.partial(lambda i, te, shift: (jnp.minimum(i + shift, t_max - 1), 0, 0), shift=shift),
        memory_space=pltpu.SMEM)
    grid_spec = pltpu.PrefetchScalarGridSpec(
        num_scalar_prefetch=1,
        grid=(t_max,),
        in_specs=[tok_spec(0), tok_spec(1), tok_spec(2),
                  pl.BlockSpec(memory_space=pl.ANY),
                  pl.BlockSpec((1, 1, d, f), lambda i, te: (layer, te[i], 0, 0)),
                  pl.BlockSpec((1, 1, d, f), lambda i, te: (layer, te[i], 0, 0)),
                  pl.BlockSpec((1, 1, f, d), lambda i, te: (layer, te[i], 0, 0))],
        out_specs=pl.BlockSpec((tm * RECORD_PITCH, LANES), lambda i, te: (i, 0)),
        scratch_shapes=[pltpu.VMEM((EXPERT_RING, tm * RECORD_PITCH, LANES), jnp.int32),
                        pltpu.SemaphoreType.DMA((EXPERT_RING,)),
                        pltpu.VMEM((d, f), BF16), pltpu.VMEM((d, f), BF16), pltpu.VMEM((f, d), BF16)],
    )
    return pl.pallas_call(
        _moe_expert_kernel,
        grid_spec=grid_spec,
        out_shape=jax.ShapeDtypeStruct((t_max * tm * RECORD_PITCH, LANES), jnp.int32),
        compiler_params=_cparams(("arbitrary",), 56),
        name="moe_experts",
    )(tile_expert, tok_slot, tok_slot, tok_slot, x_rec, w_gate, w_up, w_down)


def _moe_combine_kernel(*refs, has_next, n_steps):
    if has_next:
        (slot_ref, slot_next_ref, y_hbm, wgt_ref, hs_ref, wsd_ref, xres, gate, gpost, gnext, sh, sc,
         xout, hout, ybuf_even, ybuf_odd, sem, yacc, wbuf) = refs
    else:
        (slot_ref, slot_next_ref, y_hbm, wgt_ref, hs_ref, wsd_ref, xres, gate, gpost,
         xout, ybuf_even, ybuf_odd, sem, yacc, wbuf) = refs
    i = pl.program_id(0)
    odd = lax.rem(i, 2) == 1
    tmc = xout.shape[0]
    n_rec = TOP_K * tmc
    n_col = xout.shape[1] // LANES

    @pl.when(i == 0)
    def _():
        _start_record_gather(slot_ref, n_rec, y_hbm, ybuf_even, sem.at[0], RECORD_PITCH, COMBINE_GATHER_QUEUES)

    hs = hs_ref[...]
    for c in range(n_col):
        yacc[c] = jnp.dot(hs, wsd_ref[:, c * LANES:(c + 1) * LANES], preferred_element_type=F32)
    wgt = wgt_ref[...]
    for k in range(TOP_K):
        wbuf[k] = jnp.broadcast_to(wgt[:, k:k + 1], (tmc, LANES))
    n_iter = TOP_K // 2
    per_s = n_rec // (n_iter * REC_ROWS)

    def accumulate(cur, cur_sem, nxt, nxt_sem):
        _wait_record_gather(n_rec, y_hbm, cur, cur_sem)

        def body(kp, carry):
            ka = 2 * kp
            wa = wbuf[ka]
            wb = wbuf[ka + 1]
            first = ka * (tmc * RECORD_PITCH)
            for s in range(REC_ROWS):
                lo_a, hi_a = _unpack_bf16_pairs(cur[pl.ds(first + s, tmc, stride=RECORD_PITCH), :])
                lo_b, hi_b = _unpack_bf16_pairs(
                    cur[pl.ds(first + tmc * RECORD_PITCH + s, tmc, stride=RECORD_PITCH), :])
                yacc[s] += wa * lo_a + wb * lo_b
                yacc[REC_ROWS + s] += wa * hi_a + wb * hi_b
                for u in range(per_s):
                    r = (kp * REC_ROWS + s) * per_s + u
                    src_row = pl.multiple_of(slot_next_ref[0, 0, r] * RECORD_PITCH, 4)
                    dst_row = pl.multiple_of(r * RECORD_PITCH, 4)
                    pltpu.make_async_copy(y_hbm.at[pl.ds(src_row, REC_ROWS), :], nxt.at[pl.ds(dst_row, REC_ROWS), :],
                                          nxt_sem).start(priority=COMBINE_GATHER_QUEUES[u % len(COMBINE_GATHER_QUEUES)])
            return carry
        lax.fori_loop(0, n_iter, body, 0)

        @pl.when(i == n_steps - 1)
        def _():
            _wait_record_gather(n_rec, y_hbm, nxt, nxt_sem)

    @pl.when(jnp.logical_not(odd))
    def _():
        accumulate(ybuf_even, sem.at[0], ybuf_odd, sem.at[1])

    @pl.when(odd)
    def _():
        accumulate(ybuf_odd, sem.at[1], ybuf_even, sem.at[0])

    for r in range(tmc // 8):
        rows = slice(r * 8, (r + 1) * 8)
        y = jnp.concatenate([yacc[c, rows, :] for c in range(n_col)], axis=1)
        xn = xres[rows, :] + gate[0] * ((y * _rms_scale(y)) * gpost[...])
        xout[rows, :] = xn
        if has_next:
            h = ((xn * _rms_scale(xn)) * gnext[...]) * (1.0 + sc[0]) + sh[0]
            for c in range(n_col):
                yacc[c, rows, :] = h[:, c * LANES:(c + 1) * LANES]
    if has_next:
        hout[...] = jnp.concatenate([yacc[c] for c in range(n_col)], axis=1).astype(hout.dtype)


def _moe_combine(y_rec, slot_tiles, wgt, hs, w_sd, xres, gate, gpost, nxt):
    m, d = xres.shape
    n_steps, _, n_rec = slot_tiles.shape
    tmc = n_rec // TOP_K
    f = hs.shape[1]
    has_next = nxt is not None
    slot_spec = lambda shift: pl.BlockSpec(
        (1, 1, n_rec), functools.partial(lambda i, shift: (jnp.minimum(i + shift, n_steps - 1), 0, 0), shift=shift),
        memory_space=pltpu.SMEM)
    vec_spec = pl.BlockSpec((1, d), lambda i: (0, 0))
    mod_spec = pl.BlockSpec((1, 1, d), lambda i: (i * tmc // SEQ, 0, 0))
    row_spec = pl.BlockSpec((tmc, d), lambda i: (i, 0))
    in_specs = [slot_spec(0), slot_spec(1), pl.BlockSpec(memory_space=pl.ANY),
                pl.BlockSpec((tmc, TOP_K), lambda i: (i, 0)), pl.BlockSpec((tmc, f), lambda i: (i, 0)),
                pl.BlockSpec((f, d), lambda i: (0, 0)), row_spec, mod_spec, vec_spec]
    args = [slot_tiles, slot_tiles, y_rec, wgt, hs, w_sd, xres, gate, gpost.reshape(1, d)]
    out_specs = [row_spec]
    out_shape = [jax.ShapeDtypeStruct((m, d), F32)]
    if has_next:
        gnext, sh, sc = nxt
        in_specs += [vec_spec, mod_spec, mod_spec]
        args += [gnext.reshape(1, d), sh, sc]
        out_specs.append(row_spec)
        out_shape.append(jax.ShapeDtypeStruct((m, d), BF16))
    outs = pl.pallas_call(
        functools.partial(_moe_combine_kernel, has_next=has_next, n_steps=n_steps),
        grid=(n_steps,),
        in_specs=in_specs,
        out_specs=out_specs,
        out_shape=out_shape,
        scratch_shapes=[pltpu.VMEM((n_rec * RECORD_PITCH, LANES), jnp.int32),
                        pltpu.VMEM((n_rec * RECORD_PITCH, LANES), jnp.int32), pltpu.SemaphoreType.DMA((2,)),
                        pltpu.VMEM((d // LANES, tmc, LANES), F32), pltpu.VMEM((TOP_K, tmc, LANES), F32)],
        compiler_params=_cparams(("arbitrary",), 48),
        name="moe_combine",
    )(*args)
    return (outs[0], outs[1]) if has_next else (outs[0], None)


def _mod_parts(mods_layer, n_rows):
    return [mods_layer[:n_rows, j * D_MODEL:(j + 1) * D_MODEL].reshape(n_rows, 1, D_MODEL) for j in range(6)]


def _slot_tokens_kernel(slot_ref, out_ref):
    i = pl.program_id(0)
    ch = slot_ref.shape[1]
    unroll = 8

    @pl.when(i == 0)
    def _():
        def clear(j, carry):
            for u in range(unroll):
                out_ref[j * unroll + u] = 0
            return carry
        lax.fori_loop(0, out_ref.shape[0] // unroll, clear, 0)

    def body(j, carry):
        for u in range(unroll):
            t = j * unroll + u
            for k in range(TOP_K):
                out_ref[slot_ref[k, t]] = i * ch + t
        return carry
    lax.fori_loop(0, ch // unroll, body, 0)


def _slot_tokens(slot, n_slots, ch=1024):
    n = slot.shape[1]
    return pl.pallas_call(
        _slot_tokens_kernel,
        grid=(n // ch,),
        in_specs=[pl.BlockSpec((TOP_K, ch), lambda i: (0, i), memory_space=pltpu.SMEM)],
        out_specs=pl.BlockSpec(memory_space=pltpu.SMEM),
        out_shape=jax.ShapeDtypeStruct((n_slots,), jnp.int32),
        compiler_params=pltpu.CompilerParams(dimension_semantics=("arbitrary",)),
        name="slot_tokens",
    )(slot)


def _moe(h2, h2_rec, x1, gate2, gpost, nxt, router_w, router_bias, layer, w_gate, w_up, w_down, sh_gate, sh_up, sh_down):
    n = h2.shape[0]
    tm, tmc = MOE_TILE, COMBINE_TOKENS
    t_max = n * TOP_K // tm + N_EXPERTS
    idx, wgt, pos, counts, hs = _router(h2, router_w, router_bias, sh_gate, sh_up)
    tiles = (counts[:, 0] + tm - 1) // tm
    tile_end = jnp.cumsum(tiles)
    n_tiles = tile_end[-1]
    first_slot = (tile_end - tiles) * tm
    slot = jnp.sum(jnp.where(idx[:, :, None] == jnp.arange(N_EXPERTS)[None, None, :], first_slot[None, None, :], 0),
                   axis=-1) + pos
    tok_slot = _slot_tokens(slot.astype(jnp.int32), t_max * tm)
    tile_id = jnp.minimum(jnp.arange(t_max), n_tiles - 1)
    tile_expert = jnp.sum(tile_id[:, None] >= tile_end[None, :], axis=1).astype(jnp.int32)
    y_rec = _moe_experts(h2_rec, tile_expert, tok_slot.reshape(t_max, 1, tm), layer, w_gate, w_up, w_down)
    slot_tiles = slot.reshape(TOP_K, n // tmc, tmc).transpose(1, 0, 2).reshape(n // tmc, 1, TOP_K * tmc)
    return _moe_combine(y_rec, slot_tiles, wgt.T, hs, sh_down.astype(BF16), x1, gate2, gpost, nxt)


def kernel(x, c, ctx, c_ctx, ada_w, ada_b, norm_g, attn_w_in, attn_w_out, na_rpb, q_norm_g, k_norm_g, mix_w_in, mix_w_out, conv_w, sgu_w, sgu_b, sgu_norm_g, router_w, router_bias, moe_w_gate, moe_w_up, moe_w_down, shared_w_gate, shared_w_up, shared_w_down):
    nb, s, d = x.shape
    assert (s, d) == (SEQ, D_MODEL) and ctx.shape == (nb, CTX_LEN, d)
    x_lat = x.reshape(nb * s, d)
    x_ctx = ctx.reshape(nb * CTX_LEN, d)

    c_rows = jnp.concatenate([c, c_ctx[None, :], jnp.zeros((16 - nb - 1, d), F32)], axis=0)
    mods = _adaln(c_rows, ada_w, ada_b)
    lat_mods = [_mod_parts(mods[i], nb) for i in range(DEPTH)]
    ctx_mods = _mod_parts(mods[0, nb:nb + 1], 1)

    cos, sin = _rope_tables()
    bands, pats, na_idx = _na_static_tables()

    sh1, sc1 = lat_mods[0][0], lat_mods[0][1]
    h_lat = _norm_mod(x_lat, norm_g[0, 0], sh1, sc1, SEQ)

    for i in range(DEPTH):
        _, _, g1, sh2, sc2, g2 = lat_mods[i]
        if i % 2 == 0:
            e = i // 2
            w_in = attn_w_in[e]
            kv_lo, kv_hi = NA_DIM, 3 * NA_DIM
            gk_lo = 3 * NA_DIM + GQA_Q_DIM
            h_ctx = _norm_mod(x_ctx, norm_g[i, 0], ctx_mods[0], ctx_mods[1], nb * CTX_LEN)
            w_ctx = jnp.concatenate([w_in[:, kv_lo:kv_hi], w_in[:, gk_lo:]], axis=1).astype(BF16)
            p_lat = _matmul(h_lat, w_in.astype(BF16))
            p_ctx = _matmul(h_ctx, w_ctx)
            blk = 4 * HEAD_DIM
            qg = _qk_prep(p_lat, kv_hi // blk, GQA_Q_DIM // blk, q_norm_g[e], cos, sin, True)
            kg = _qk_prep(p_lat, gk_lo // blk, GQA_KV_DIM // blk, k_norm_g[e], cos, sin, True)
            kcg = _qk_prep(p_ctx, 2 * NA_DIM // blk, GQA_KV_DIM // blk, k_norm_g[e], cos, sin, False)
            bias = _na_bias_table(na_rpb[e], na_idx)
            o_na = _na_attention(p_lat, p_ctx, bias, bands, pats)
            o_gqa = _gqa_attention(qg, kg, p_lat, kcg, p_ctx,
                                   (gk_lo + GQA_KV_DIM) // HEAD_DIM, (2 * NA_DIM + GQA_KV_DIM) // HEAD_DIM)
            parts = [o_na, o_gqa]
            w_out = attn_w_out[e].astype(BF16)
        else:
            o = i // 2
            p_mix = _matmul(h_lat, mix_w_in[o].astype(BF16))
            parts = [_conv_sgu(p_mix, conv_w[o], sgu_w[o], sgu_b[o], sgu_norm_g[o])]
            wo = mix_w_out[o]
            w_out = jnp.concatenate(
                [wo[:CONV_DIM].reshape(SGU_GROUPS, SGU_CH, d), wo[CONV_DIM:].reshape(SGU_GROUPS, SGU_CH, d)],
                axis=1).reshape(CONV_DIM + SGU_DIM, d).astype(BF16)
        x1, h2, h2_rec = _matmul_post(parts, w_out, x_lat, g1, norm_g[i, 1], norm_g[i, 2], sh2, sc2)
        nxt = None
        if i + 1 < DEPTH:
            nxt = (norm_g[i + 1, 0], lat_mods[i + 1][0], lat_mods[i + 1][1])
        x_lat, h_lat = _moe(h2, h2_rec, x1, g2, norm_g[i, 3], nxt, router_w[i], router_bias[i],
                            i, moe_w_gate, moe_w_up, moe_w_down,
                            shared_w_gate[i], shared_w_up[i], shared_w_down[i])
    return x_lat.reshape(nb, s, d)
```
